```python
import math
import jax, jax.numpy as jnp
from jax import lax
import numpy as np

D_MODEL = 2048
BATCH = 4
SEQ = 4096
DEPTH = 2

N_MIXERS = 2
RMS_EPS = 1e-6
LN_EPS = 1e-5
CONV_WIDTH = 31
DA_HEADS = 8
DA_HEAD_DIM = D_MODEL // (2 * DA_HEADS)
DA_V_DIM = 2 * DA_HEAD_DIM
ROPE_THETA = 500000.0
ROT_DIM = DA_HEAD_DIM // 4
Q_BLOCK = 128
D_FF = 5504
FFN_CONV_WIDTH = 3

N_CONF = (DEPTH + 1) // 2
N_ATTN = DEPTH // 2

kernel_name = "hybrid_conformer_diffattn_convffn"


def rms_norm(x, g):
    xf = x.astype(jnp.float32)
    y = xf * lax.rsqrt(jnp.mean(xf * xf, axis=-1, keepdims=True) + RMS_EPS)
    return y.astype(x.dtype) * g


def layer_norm(x, g, b):
    xf = x.astype(jnp.float32)
    mu = jnp.mean(xf, axis=-1, keepdims=True)
    var = jnp.mean(jnp.square(xf - mu), axis=-1, keepdims=True)
    y = (xf - mu) * lax.rsqrt(var + LN_EPS)
    return y.astype(x.dtype) * g + b


def causal_dwconv(x, w, b):
    k = w.shape[0]
    c = x.shape[-1]
    y = lax.conv_general_dilated(
        x, w[:, None, :].astype(x.dtype), window_strides=(1,),
        padding=[(k - 1, 0)], dimension_numbers=("NWC", "WIO", "NWC"),
        feature_group_count=c)
    return y + b


def partial_rope(x, positions):
    half = ROT_DIM // 2
    inv_freq = ROPE_THETA ** (-jnp.arange(half, dtype=jnp.float32) / half)
    ang = positions.astype(jnp.float32)[..., None] * inv_freq
    cos = jnp.cos(ang)[:, :, None, :]
    sin = jnp.sin(ang)[:, :, None, :]
    xr = x[..., :ROT_DIM].astype(jnp.float32)
    x1, x2 = xr[..., :half], xr[..., half:]
    rot = jnp.concatenate([x1 * cos - x2 * sin, x2 * cos + x1 * sin], axis=-1)
    return jnp.concatenate([rot.astype(x.dtype), x[..., ROT_DIM:]], axis=-1)


def conformer_conv(x, w_in, b_in, dw, dw_b, ln_g, ln_b, w_out, b_out):
    h = x @ w_in + b_in
    a, gate = jnp.split(h, 2, axis=-1)
    h = a * jax.nn.sigmoid(gate)
    h = causal_dwconv(h, dw, dw_b)
    h = jax.nn.silu(layer_norm(h, ln_g, ln_b))
    return h @ w_out + b_out


def diff_attention(x, positions, w_qkv, lq1, lk1, lq2, lk2, subln_g, w_o, layer_idx):
    b, s, _ = x.shape
    qkv = x @ w_qkv
    q, k, v = jnp.split(qkv, [2 * DA_HEADS * DA_HEAD_DIM, 4 * DA_HEADS * DA_HEAD_DIM], axis=-1)
    q = q.reshape(b, s, 2 * DA_HEADS, DA_HEAD_DIM)
    k = k.reshape(b, s, 2 * DA_HEADS, DA_HEAD_DIM)
    v = v.reshape(b, s, DA_HEADS, DA_V_DIM)
    q = partial_rope(q, positions) * (DA_HEAD_DIM ** -0.5)
    k = partial_rope(k, positions)

    lam_init = 0.8 - 0.6 * math.exp(-0.3 * layer_idx)
    lam = (jnp.exp(jnp.sum(lq1.astype(jnp.float32) * lk1.astype(jnp.float32)))
           - jnp.exp(jnp.sum(lq2.astype(jnp.float32) * lk2.astype(jnp.float32)))
           + lam_init)

    n_blk = s // Q_BLOCK
    q_blocks = q.reshape(b, n_blk, Q_BLOCK, 2 * DA_HEADS, DA_HEAD_DIM).transpose(1, 0, 2, 3, 4)
    key_idx = jnp.arange(s)
    neg = jnp.finfo(jnp.float32).min

    def one_block(args):
        qi, blk = args
        sc = jnp.einsum("bqhd,bkhd->bhqk", qi, k).astype(jnp.float32)
        q_idx = blk * Q_BLOCK + jnp.arange(Q_BLOCK)
        mask = key_idx[None, :] <= q_idx[:, None]
        sc = jnp.where(mask, sc, neg)
        p = jax.nn.softmax(sc, axis=-1).reshape(b, DA_HEADS, 2, Q_BLOCK, s)
        a = p[:, :, 0] - lam * p[:, :, 1]
        return jnp.einsum("bhqk,bkhe->bqhe", a.astype(v.dtype), v)

    o = lax.map(one_block, (q_blocks, jnp.arange(n_blk)))
    o = o.transpose(1, 0, 2, 3, 4).reshape(b, s, DA_HEADS, DA_V_DIM)
    o = rms_norm(o, subln_g) * (1.0 - lam_init)
    return o.reshape(b, s, DA_HEADS * DA_V_DIM) @ w_o


def conv_ffn(x, w_gate, w_up, dw, dw_b, w_down):
    g = causal_dwconv(x @ w_gate, dw, dw_b)
    return (jax.nn.silu(g) * (x @ w_up)) @ w_down


def setup_inputs(seed: int = 0) -> dict:
    key = jax.random.key(seed)
    ks = jax.random.split(key, 32)
    f32 = jnp.float32
    D, F = D_MODEL, D_FF

    def nrm(k, shape, scale):
        return jax.random.normal(k, shape, f32) * scale

    x = jax.random.normal(ks[0], (BATCH, SEQ, D), f32)
    positions = jnp.broadcast_to(jnp.arange(SEQ, dtype=jnp.int32)[None, :], (BATCH, SEQ))
    return {
        "x": x,
        "positions": positions,
        "norm_mix": 1.0 + nrm(ks[1], (DEPTH, D), 0.01),
        "norm_ffn": 1.0 + nrm(ks[2], (DEPTH, D), 0.01),
        "norm_final": 1.0 + nrm(ks[3], (D,), 0.01),
        "conv_w_in": nrm(ks[4], (N_CONF, D, 2 * D), D ** -0.5),
        "conv_b_in": nrm(ks[5], (N_CONF, 2 * D), 0.01),
        "conv_dw": nrm(ks[6], (N_CONF, CONV_WIDTH, D), CONV_WIDTH ** -0.5),
        "conv_dw_b": nrm(ks[7], (N_CONF, D), 0.01),
        "conv_ln_g": 1.0 + nrm(ks[8], (N_CONF, D), 0.01),
        "conv_ln_b": nrm(ks[9], (N_CONF, D), 0.01),
        "conv_w_out": nrm(ks[10], (N_CONF, D, D), D ** -0.5),
        "conv_b_out": nrm(ks[11], (N_CONF, D), 0.01),
        "attn_w_qkv": nrm(ks[12], (N_ATTN, D, 2 * DA_HEADS * DA_HEAD_DIM * 2 + DA_HEADS * DA_V_DIM), D ** -0.5),
        "attn_lq1": nrm(ks[13], (N_ATTN, DA_HEAD_DIM), 0.1),
        "attn_lk1": nrm(ks[14], (N_ATTN, DA_HEAD_DIM), 0.1),
        "attn_lq2": nrm(ks[15], (N_ATTN, DA_HEAD_DIM), 0.1),
        "attn_lk2": nrm(ks[16], (N_ATTN, DA_HEAD_DIM), 0.1),
        "attn_subln_g": 1.0 + nrm(ks[17], (N_ATTN, DA_V_DIM), 0.01),
        "attn_w_o": nrm(ks[18], (N_ATTN, DA_HEADS * DA_V_DIM, D), (DA_HEADS * DA_V_DIM) ** -0.5),
        "ffn_w_gate": nrm(ks[19], (DEPTH, D, F), D ** -0.5),
        "ffn_w_up": nrm(ks[20], (DEPTH, D, F), D ** -0.5),
        "ffn_dw": nrm(ks[21], (DEPTH, FFN_CONV_WIDTH, F), FFN_CONV_WIDTH ** -0.5),
        "ffn_dw_b": nrm(ks[22], (DEPTH, F), 0.01),
        "ffn_w_down": nrm(ks[23], (DEPTH, F, D), F ** -0.5),
    }


def reference(x, positions, norm_mix, norm_ffn, norm_final,
              conv_w_in, conv_b_in, conv_dw, conv_dw_b, conv_ln_g, conv_ln_b,
              conv_w_out, conv_b_out,
              attn_w_qkv, attn_lq1, attn_lk1, attn_lq2, attn_lk2, attn_subln_g, attn_w_o,
              ffn_w_gate, ffn_w_up, ffn_dw, ffn_dw_b, ffn_w_down):
    for i in range(DEPTH):
        h = rms_norm(x, norm_mix[i])
        j = i // N_MIXERS
        if i % N_MIXERS == 0:
            mix = conformer_conv(h, conv_w_in[j], conv_b_in[j], conv_dw[j], conv_dw_b[j],
                                 conv_ln_g[j], conv_ln_b[j], conv_w_out[j], conv_b_out[j])
        else:
            mix = diff_attention(h, positions, attn_w_qkv[j], attn_lq1[j], attn_lk1[j],
                                 attn_lq2[j], attn_lk2[j], attn_subln_g[j], attn_w_o[j], i)
        x = x + mix
        h = rms_norm(x, norm_ffn[i])
        x = x + conv_ffn(h, ffn_w_gate[i], ffn_w_up[i], ffn_dw[i], ffn_dw_b[i], ffn_w_down[i])
    return rms_norm(x, norm_final)
```

```python
import functools
import math

import jax
import jax.numpy as jnp
from jax import lax
from jax.experimental import pallas as pl
from jax.experimental.pallas import tpu as pltpu

RMS_EPS = 1e-6
LN_EPS = 1e-5
ROPE_THETA = 500000.0
N_MIXERS = 2

V7X_LANES = 128
V7X_MXU_DIM = 256
V7X_VMEM_LIMIT_BYTES = 56 * 1024 * 1024

F32 = jnp.float32
BF16 = jnp.bfloat16
MASK_VALUE = -1e30

CONV_HALO = 32
CONV_ROWS = 32
CONV_LANES = 256
FFN_HALO = 8


def _pick(n, candidates):
    for c in candidates:
        if n % c == 0:
            return c
    raise ValueError(f"no tile in {candidates} divides {n}")


def _params(*sem):
    return pltpu.CompilerParams(dimension_semantics=sem,
                                vmem_limit_bytes=V7X_VMEM_LIMIT_BYTES)


def _rms(x, g):
    ms = jnp.mean(x * x, axis=-1, keepdims=True)
    return x * lax.rsqrt(ms + RMS_EPS) * g


def _dot(a, b):
    return jnp.dot(a, b, preferred_element_type=F32)


def _conf_in_kernel(x_ref, ng_ref, wa_ref, wb_ref, ba_ref, bb_ref, o_ref, hn_ref):
    @pl.when(pl.program_id(1) == 0)
    def _():
        hn_ref[...] = _rms(x_ref[...], ng_ref[...]).astype(BF16)

    hn = hn_ref[...]
    a = _dot(hn, wa_ref[...]) + ba_ref[...]
    gate = _dot(hn, wb_ref[...]) + bb_ref[...]
    o_ref[...] = a * jax.nn.sigmoid(gate)


def _conf_in(x2, norm_g, w_in, b_in, *, tm, tn):
    m, d = x2.shape
    nj = d // tn
    return pl.pallas_call(
        _conf_in_kernel,
        grid=(m // tm, nj),
        in_specs=[
            pl.BlockSpec((tm, d), lambda i, j: (i, 0)),
            pl.BlockSpec((1, d), lambda i, j: (0, 0)),
            pl.BlockSpec((d, tn), lambda i, j: (0, j)),
            pl.BlockSpec((d, tn), lambda i, j: (0, j + nj)),
            pl.BlockSpec((1, tn), lambda i, j: (0, j)),
            pl.BlockSpec((1, tn), lambda i, j: (0, j + nj)),
        ],
        out_specs=pl.BlockSpec((tm, tn), lambda i, j: (i, j)),
        out_shape=jax.ShapeDtypeStruct((m, d), F32),
        scratch_shapes=[pltpu.VMEM((tm, d), BF16)],
        compiler_params=_params("arbitrary", "arbitrary"),
        name="conf_in",
    )(x2, norm_g.reshape(1, d), w_in, w_in, b_in.reshape(1, 2 * d), b_in.reshape(1, 2 * d))


def _conf_out_kernel(glu_ref, halo_ref, x_ref, dw_ref, dwb_ref, lng_ref, lnb_ref,
                     wo_ref, bo_ref, o_ref, cbuf_ref, h_ref, y_ref, *, tm, seq_len, width):
    d = glu_ref.shape[1]
    seq_start = (pl.program_id(0) * tm) % seq_len == 0

    @pl.when(seq_start)
    def _():
        cbuf_ref[0:CONV_HALO, :] = jnp.zeros((CONV_HALO, d), F32)

    @pl.when(jnp.logical_not(seq_start))
    def _():
        cbuf_ref[0:CONV_HALO, :] = halo_ref[...]

    cbuf_ref[CONV_HALO:, :] = glu_ref[...]

    base = CONV_HALO - (width - 1)

    def lane_block(c, carry):
        lanes = pl.ds(pl.multiple_of(c * CONV_LANES, CONV_LANES), CONV_LANES)
        for rb in range(tm // CONV_ROWS):
            r0 = rb * CONV_ROWS
            acc = jnp.zeros((CONV_ROWS, CONV_LANES), F32)
            for k in range(width):
                acc = acc + dw_ref[k:k + 1, lanes] * cbuf_ref[r0 + base + k:r0 + base + k + CONV_ROWS, lanes]
            h_ref[r0:r0 + CONV_ROWS, lanes] = acc + dwb_ref[:, lanes]
        return carry

    lax.fori_loop(0, d // CONV_LANES, lane_block, 0)

    def row_block(rb, carry):
        r0 = pl.multiple_of(rb * CONV_ROWS, CONV_ROWS)
        h = h_ref[pl.ds(r0, CONV_ROWS), :]
        mu = jnp.mean(h, axis=-1, keepdims=True)
        hc = h - mu
        var = jnp.mean(hc * hc, axis=-1, keepdims=True)
        y = hc * lax.rsqrt(var + LN_EPS) * lng_ref[...] + lnb_ref[...]
        y_ref[pl.ds(r0, CONV_ROWS), :] = (y * jax.nn.sigmoid(y)).astype(BF16)
        return carry

    lax.fori_loop(0, tm // CONV_ROWS, row_block, 0)
    o_ref[...] = x_ref[...] + _dot(y_ref[...], wo_ref[...]) + bo_ref[...]


def _conf_out(glu, x2, dw, dwb, ln_g, ln_b, w_out, b_out, *, tm, seq_len):
    m, d = x2.shape
    width = dw.shape[0]
    assert width - 1 <= CONV_HALO and tm % CONV_HALO == 0 and tm % CONV_ROWS == 0
    assert d % CONV_LANES == 0
    hb = tm // CONV_HALO
    row = lambda a: a.reshape(1, d)
    kern = functools.partial(_conf_out_kernel, tm=tm, seq_len=seq_len, width=width)
    return pl.pallas_call(
        kern,
        grid=(m // tm,),
        in_specs=[
            pl.BlockSpec((tm, d), lambda i: (i, 0)),
            pl.BlockSpec((CONV_HALO, d), lambda i: (jnp.maximum(i * hb - 1, 0), 0)),
            pl.BlockSpec((tm, d), lambda i: (i, 0)),
            pl.BlockSpec((width, d), lambda i: (0, 0)),
            pl.BlockSpec((1, d), lambda i: (0, 0)),
            pl.BlockSpec((1, d), lambda i: (0, 0)),
            pl.BlockSpec((1, d), lambda i: (0, 0)),
            pl.BlockSpec((d, d), lambda i: (0, 0)),
            pl.BlockSpec((1, d), lambda i: (0, 0)),
        ],
        out_specs=pl.BlockSpec((tm, d), lambda i: (i, 0)),
        out_shape=jax.ShapeDtypeStruct((m, d), F32),
        scratch_shapes=[pltpu.VMEM((CONV_HALO + tm, d), F32), pltpu.VMEM((tm, d), F32),
                        pltpu.VMEM((tm, d), BF16)],
        compiler_params=_params("arbitrary"),
        name="conf_out",
    )(glu, glu, x2, dw, row(dwb), row(ln_g), row(ln_b), w_out, row(b_out))


def _ffn_kernel(x_ref, ng_ref, wg_ref, wu_ref, dw_ref, dwb_ref, wd_ref, fg_ref, o_ref,
                hn_ref, gbuf_ref, carry_ref, *, tm, seq_len, final_norm):
    i = pl.program_id(0)
    j = pl.program_id(1)

    @pl.when(j == 0)
    def _():
        x = x_ref[...]
        hn_ref[...] = _rms(x, ng_ref[...]).astype(BF16)
        o_ref[...] = x

    hn = hn_ref[...]
    g_lin = _dot(hn, wg_ref[...])
    up = _dot(hn, wu_ref[...])
    seq_start = (i * tm) % seq_len == 0

    @pl.when(seq_start)
    def _():
        gbuf_ref[0:FFN_HALO, :] = jnp.zeros((FFN_HALO, g_lin.shape[1]), F32)

    @pl.when(jnp.logical_not(seq_start))
    def _():
        gbuf_ref[0:FFN_HALO, :] = carry_ref[j]

    gbuf_ref[FFN_HALO:, :] = g_lin
    carry_ref[j] = g_lin[tm - FFN_HALO:, :]
    g = (dw_ref[2:3, :] * g_lin
         + dw_ref[1:2, :] * gbuf_ref[FFN_HALO - 1:FFN_HALO - 1 + tm, :]
         + dw_ref[0:1, :] * gbuf_ref[FFN_HALO - 2:FFN_HALO - 2 + tm, :]
         + dwb_ref[...])
    act = (g * jax.nn.sigmoid(g) * up).astype(BF16)
    o_ref[...] += _dot(act, wd_ref[...])

    if final_norm:
        @pl.when(j == pl.num_programs(1) - 1)
        def _():
            o_ref[...] = _rms(o_ref[...], fg_ref[...])


def _ffn(x2, norm_g, w_gate, w_up, dw, dwb, w_down, final_g, *, tm, tf, seq_len):
    m, d = x2.shape
    fp = w_gate.shape[1]
    assert dw.shape[0] == 3 and fp % tf == 0
    nj = fp // tf
    final_norm = final_g is not None
    fg = (final_g if final_norm else jnp.ones((d,), F32)).reshape(1, d)
    kern = functools.partial(_ffn_kernel, tm=tm, seq_len=seq_len, final_norm=final_norm)
    return pl.pallas_call(
        kern,
        grid=(m // tm, nj),
        in_specs=[
            pl.BlockSpec((tm, d), lambda i, j: (i, 0)),
            pl.BlockSpec((1, d), lambda i, j: (0, 0)),
            pl.BlockSpec((d, tf), lambda i, j: (0, j)),
            pl.BlockSpec((d, tf), lambda i, j: (0, j)),
            pl.BlockSpec((3, tf), lambda i, j: (0, j)),
            pl.BlockSpec((1, tf), lambda i, j: (0, j)),
            pl.BlockSpec((tf, d), lambda i, j: (j, 0)),
            pl.BlockSpec((1, d), lambda i, j: (0, 0)),
        ],
        out_specs=pl.BlockSpec((tm, d), lambda i, j: (i, 0)),
        out_shape=jax.ShapeDtypeStruct((m, d), F32),
        scratch_shapes=[
            pltpu.VMEM((tm, d), BF16),
            pltpu.VMEM((FFN_HALO + tm, tf), F32),
            pltpu.VMEM((nj, FFN_HALO, tf), F32),
        ],
        compiler_params=_params("arbitrary", "arbitrary"),
        name="ffn",
    )(x2, norm_g.reshape(1, d), w_gate, w_up, dw, dwb.reshape(1, fp), w_down, fg)


def _qkv_kernel(x_ref, ng_ref, pos_ref, invf_ref, w_ref, o_ref, hn_ref, cos_ref, sa_ref, sb_ref,
                *, n_q_tiles, n_k_tiles, head_dim, half, q_scale):
    j = pl.program_id(1)
    tn = o_ref.shape[1]

    @pl.when(j == 0)
    def _():
        hn_ref[...] = _rms(x_ref[...], ng_ref[...]).astype(BF16)
        ang = pos_ref[...].astype(F32) * invf_ref[...]
        sin = jnp.sin(ang)
        lane = lax.broadcasted_iota(jnp.int32, ang.shape, 1)
        cos_ref[...] = jnp.cos(ang)
        sa_ref[...] = jnp.where((lane >= half) & (lane < 2 * half), sin, 0.0)
        sb_ref[...] = jnp.where(lane < half, -sin, 0.0)

    acc = _dot(hn_ref[...], w_ref[...])

    def rope(scale):
        cos, sa, sb = cos_ref[...], sa_ref[...], sb_ref[...]
        outs = []
        for h in range(tn // head_dim):
            seg = acc[:, h * head_dim:(h + 1) * head_dim]
            r = (seg * cos + pltpu.roll(seg, half, 1) * sa
                 + pltpu.roll(seg, head_dim - half, 1) * sb)
            outs.append(r if scale is None else r * scale)
        return jnp.concatenate(outs, axis=1)

    @pl.when(j < n_q_tiles)
    def _():
        o_ref[...] = rope(q_scale).astype(BF16)

    @pl.when((j >= n_q_tiles) & (j < n_q_tiles + n_k_tiles))
    def _():
        o_ref[...] = rope(None).astype(BF16)

    @pl.when(j >= n_q_tiles + n_k_tiles)
    def _():
        o_ref[...] = acc.astype(BF16)


def _qkv(x2, norm_g, positions, w_qkv, *, tm, tn, head_dim, qk_width):
    m, d = x2.shape
    n = w_qkv.shape[1]
    assert head_dim == V7X_LANES and qk_width % tn == 0 and n % tn == 0
    rot = head_dim // 4
    half = rot // 2
    inv_freq = ROPE_THETA ** (-jnp.arange(half, dtype=F32) / half)
    invf = jnp.zeros((head_dim,), F32).at[:rot].set(jnp.tile(inv_freq, 2)).reshape(1, head_dim)
    kern = functools.partial(_qkv_kernel, n_q_tiles=qk_width // tn, n_k_tiles=qk_width // tn,
                             head_dim=head_dim, half=half, q_scale=head_dim ** -0.5)
    return pl.pallas_call(
        kern,
        grid=(m // tm, n // tn),
        in_specs=[
            pl.BlockSpec((tm, d), lambda i, j: (i, 0)),
            pl.BlockSpec((1, d), lambda i, j: (0, 0)),
            pl.BlockSpec((tm, 1), lambda i, j: (i, 0)),
            pl.BlockSpec((1, head_dim), lambda i, j: (0, 0)),
            pl.BlockSpec((d, tn), lambda i, j: (0, j)),
        ],
        out_specs=pl.BlockSpec((tm, tn), lambda i, j: (i, j)),
        out_shape=jax.ShapeDtypeStruct((m, n), BF16),
        scratch_shapes=[pltpu.VMEM((tm, d), BF16)] + [pltpu.VMEM((tm, head_dim), F32)] * 3,
        compiler_params=_params("arbitrary", "arbitrary"),
        name="qkv",
    )(x2, norm_g.reshape(1, d), positions.reshape(m, 1), invf, w_qkv)


def _attn_kernel(q_ref, k_ref, v_ref, lq1_ref, lk1_ref, lq2_ref, lk2_ref, sg_ref, o_ref,
                 m_ref, l_ref, acc_ref, *, tq, head_dim, lam_init):
    qi = pl.program_id(2)
    m_ref[...] = jnp.full(m_ref.shape, MASK_VALUE, F32)
    l_ref[...] = jnp.zeros(l_ref.shape, F32)
    acc_ref[...] = jnp.zeros(acc_ref.shape, F32)

    def block(kj, masked):
        k0 = pl.multiple_of(kj * tq, tq)
        v = v_ref[pl.ds(k0, tq), :]
        for idx in range(2):
            cols = slice(idx * head_dim, (idx + 1) * head_dim)
            q = q_ref[:, cols]
            k = k_ref[pl.ds(k0, tq), cols]
            s = lax.dot_general(q, k, (((1,), (1,)), ((), ())), preferred_element_type=F32)
            if masked:
                row = lax.broadcasted_iota(jnp.int32, s.shape, 0)
                col = lax.broadcasted_iota(jnp.int32, s.shape, 1)
                s = jnp.where(col <= row, s, MASK_VALUE)
            m_prev = m_ref[idx]
            m_new = jnp.maximum(m_prev, jnp.max(s, axis=1, keepdims=True))
            alpha = jnp.exp(m_prev - m_new)
            p = jnp.exp(s - m_new)
            l_ref[idx] = alpha * l_ref[idx] + jnp.sum(p, axis=1, keepdims=True)
            acc_ref[idx] = alpha * acc_ref[idx] + _dot(p.astype(BF16), v)
            m_ref[idx] = m_new

    def body(kj, carry):
        block(kj, False)
        return carry

    lax.fori_loop(0, qi, body, 0)
    block(qi, True)

    lam = (jnp.exp(jnp.sum(lq1_ref[...] * lk1_ref[...], keepdims=True))
           - jnp.exp(jnp.sum(lq2_ref[...] * lk2_ref[...], keepdims=True)) + lam_init)
    o = acc_ref[0] / l_ref[0] - lam * (acc_ref[1] / l_ref[1])
    o_ref[...] = (_rms(o, sg_ref[...]) * (1.0 - lam_init)).astype(BF16)


def _attn(qkv, lq1, lk1, lq2, lk2, subln_g, *, batch, seq_len, heads, head_dim, tq, lam_init):
    m = qkv.shape[0]
    vd = 2 * head_dim
    nq = seq_len // tq
    k_col0 = heads
    v_col0 = 2 * heads
    vec = lambda a: a.reshape(1, head_dim)
    kern = functools.partial(_attn_kernel, tq=tq, head_dim=head_dim, lam_init=lam_init)
    small = pl.BlockSpec((1, head_dim), lambda b, h, q: (0, 0))
    return pl.pallas_call(
        kern,
        grid=(batch, heads, nq),
        in_specs=[
            pl.BlockSpec((tq, vd), lambda b, h, q: (b * nq + q, h)),
            pl.BlockSpec((seq_len, vd), lambda b, h, q: (b, k_col0 + h)),
            pl.BlockSpec((seq_len, vd), lambda b, h, q: (b, v_col0 + h)),
            small, small, small, small,
            pl.BlockSpec((1, vd), lambda b, h, q: (0, 0)),
        ],
        out_specs=pl.BlockSpec((tq, vd), lambda b, h, q: (b * nq + q, h)),
        out_shape=jax.ShapeDtypeStruct((m, heads * vd), BF16),
        scratch_shapes=[
            pltpu.VMEM((2, tq, 1), F32),
            pltpu.VMEM((2, tq, 1), F32),
            pltpu.VMEM((2, tq, vd), F32),
        ],
        compiler_params=_params("arbitrary", "arbitrary", "arbitrary"),
        name="diff_attn",
    )(qkv, qkv, qkv, vec(lq1), vec(lk1), vec(lq2), vec(lk2), subln_g.reshape(1, vd))


def _proj_res_kernel(a_ref, w_ref, r_ref, o_ref):
    o_ref[...] = r_ref[...] + _dot(a_ref[...], w_ref[...])


def _proj_res(a, w, res, *, tm):
    m, k = a.shape
    n = w.shape[1]
    return pl.pallas_call(
        _proj_res_kernel,
        grid=(m // tm,),
        in_specs=[
            pl.BlockSpec((tm, k), lambda i: (i, 0)),
            pl.BlockSpec((k, n), lambda i: (0, 0)),
            pl.BlockSpec((tm, n), lambda i: (i, 0)),
        ],
        out_specs=pl.BlockSpec((tm, n), lambda i: (i, 0)),
        out_shape=jax.ShapeDtypeStruct((m, n), F32),
        compiler_params=_params("arbitrary"),
        name="attn_out",
    )(a, w, res)


def _pad_cols(a, n):
    return a if a.shape[-1] == n else jnp.pad(a, [(0, 0)] * (a.ndim - 1) + [(0, n - a.shape[-1])])


def kernel(x, positions, norm_mix, norm_ffn, norm_final, conv_w_in, conv_b_in, conv_dw, conv_dw_b, conv_ln_g, conv_ln_b, conv_w_out, conv_b_out, attn_w_qkv, attn_lq1, attn_lk1, attn_lq2, attn_lk2, attn_subln_g, attn_w_o, ffn_w_gate, ffn_w_up, ffn_dw, ffn_dw_b, ffn_w_down):
    batch, seq_len, d = x.shape
    depth = norm_mix.shape[0]
    m = batch * seq_len
    d_ff = ffn_w_gate.shape[2]
    head_dim = attn_lq1.shape[1]
    heads = d // (2 * head_dim)
    qk_width = 2 * heads * head_dim

    tm = _pick(seq_len, (512, 256, 128, 64, 32))
    tm_conv = _pick(seq_len, (256, 128, 64, 32))
    tn = _pick(d, (512, 256, 128))
    tq = _pick(seq_len, (512, 256, 128))
    tf = 2 * V7X_MXU_DIM
    f_pad = -(-d_ff // tf) * tf

    x2 = x.reshape(m, d)
    for i in range(depth):
        j = i // N_MIXERS
        if i % N_MIXERS == 0:
            glu = _conf_in(x2, norm_mix[i], conv_w_in[j].astype(BF16), conv_b_in[j], tm=tm, tn=tn)
            x2 = _conf_out(glu, x2, conv_dw[j], conv_dw_b[j], conv_ln_g[j], conv_ln_b[j],
                           conv_w_out[j].astype(BF16), conv_b_out[j], tm=tm_conv, seq_len=seq_len)
        else:
            lam_init = 0.8 - 0.6 * math.exp(-0.3 * i)
            qkv = _qkv(x2, norm_mix[i], positions, attn_w_qkv[j].astype(BF16),
                       tm=tm, tn=tn, head_dim=head_dim, qk_width=qk_width)
            o = _attn(qkv, attn_lq1[j], attn_lk1[j], attn_lq2[j], attn_lk2[j], attn_subln_g[j],
                      batch=batch, seq_len=seq_len, heads=heads, head_dim=head_dim, tq=tq,
                      lam_init=lam_init)
            x2 = _proj_res(o, attn_w_o[j].astype(BF16), x2, tm=tm)
        final_g = norm_final if i == depth - 1 else None
        x2 = _ffn(x2, norm_ffn[i],
                  _pad_cols(ffn_w_gate[i], f_pad).astype(BF16),
                  _pad_cols(ffn_w_up[i], f_pad).astype(BF16),
                  _pad_cols(ffn_dw[i], f_pad), _pad_cols(ffn_dw_b[i], f_pad),
                  jnp.pad(ffn_w_down[i], ((0, f_pad - d_ff), (0, 0))).astype(BF16),
                  final_g, tm=tm, tf=tf, seq_len=seq_len)
    if depth == 0:
        raise ValueError("depth must be positive")
    return x2.reshape(batch, seq_len, d)
```

```python
import functools
import math

import jax
import jax.numpy as jnp
from jax import lax
from jax.experimental import pallas as pl
from jax.experimental.pallas import tpu as pltpu

RMS_EPS = 1e-6
LN_EPS = 1e-5
ROPE_THETA = 500000.0
N_MIXERS = 2

V7X_LANES = 128
SUBLANES = 8
V7X_MXU_DIM = 256
V7X_VMEM_LIMIT_BYTES = 60 * 1024 * 1024

F32 = jnp.float32
BF16 = jnp.bfloat16
MASK_VALUE = -1e30

CONV_HALO = 32
CONV_ROWS = 32
CONV_LANES = 256
FFN_HALO = 8


def _pick(n, candidates):
    for c in candidates:
        if n % c == 0:
            return c
    raise ValueError(f"no tile in {candidates} divides {n}")


def _params(*sem):
    return pltpu.CompilerParams(dimension_semantics=sem,
                                vmem_limit_bytes=V7X_VMEM_LIMIT_BYTES)


def _rms(x, g):
    ms = jnp.mean(x * x, axis=-1, keepdims=True)
    return x * lax.rsqrt(ms + RMS_EPS) * g


def _dot(a, b):
    return jnp.dot(a, b, preferred_element_type=F32)


def _conf_in_kernel(x_ref, ng_ref, wa_ref, wb_ref, ba_ref, bb_ref, o_ref, hn_ref):
    @pl.when(pl.program_id(1) == 0)
    def _():
        hn_ref[...] = _rms(x_ref[...], ng_ref[...]).astype(BF16)

    hn = hn_ref[...]
    for c in range(o_ref.shape[1] // V7X_MXU_DIM):
        cols = slice(c * V7X_MXU_DIM, (c + 1) * V7X_MXU_DIM)
        a = _dot(hn, wa_ref[:, cols]) + ba_ref[:, cols]
        gate = _dot(hn, wb_ref[:, cols]) + bb_ref[:, cols]
        o_ref[:, cols] = a * jax.nn.sigmoid(gate)


def _conf_in(x2, norm_g, w_in, b_in, *, tm, tn):
    m, d = x2.shape
    nj = d // tn
    return pl.pallas_call(
        _conf_in_kernel,
        grid=(m // tm, nj),
        in_specs=[
            pl.BlockSpec((tm, d), lambda i, j: (i, 0)),
            pl.BlockSpec((1, d), lambda i, j: (0, 0)),
            pl.BlockSpec((d, tn), lambda i, j: (0, j)),
            pl.BlockSpec((d, tn), lambda i, j: (0, j + nj)),
            pl.BlockSpec((1, tn), lambda i, j: (0, j)),
            pl.BlockSpec((1, tn), lambda i, j: (0, j + nj)),
        ],
        out_specs=pl.BlockSpec((tm, tn), lambda i, j: (i, j)),
        out_shape=jax.ShapeDtypeStruct((m, d), F32),
        scratch_shapes=[pltpu.VMEM((tm, d), BF16)],
        compiler_params=_params("arbitrary", "arbitrary"),
        name="conf_in",
    )(x2, norm_g.reshape(1, d), w_in, w_in, b_in.reshape(1, 2 * d), b_in.reshape(1, 2 * d))


def _conf_out_kernel(glu_ref, halo_ref, x_ref, dw_ref, dwb_ref, lng_ref, lnb_ref,
                     wo_ref, bo_ref, o_ref, cbuf_ref, xs_ref, h_ref, y_ref, *, tm, seq_len, width):
    d = glu_ref.shape[1]
    seq_start = (pl.program_id(0) * tm) % seq_len == 0

    @pl.when(seq_start)
    def _():
        cbuf_ref[0:CONV_HALO, :] = jnp.zeros((CONV_HALO, d), F32)

    @pl.when(jnp.logical_not(seq_start))
    def _():
        cbuf_ref[0:CONV_HALO, :] = halo_ref[...]

    cbuf_ref[CONV_HALO:, :] = glu_ref[...]

    base = CONV_HALO - (width - 1)
    def lane_block(c, carry):
        lanes = pl.ds(pl.multiple_of(c * CONV_LANES, CONV_LANES), CONV_LANES)
        for s in range(min(SUBLANES, width)):
            span = tm + SUBLANES * ((width - 1 - s) // SUBLANES)
            xs_ref[s, 0:span, :] = cbuf_ref[base + s:base + s + span, lanes]
        for rb in range(tm // CONV_ROWS):
            r0 = rb * CONV_ROWS
            acc = jnp.zeros((CONV_ROWS, CONV_LANES), F32)
            for k in range(width):
                a, s = divmod(k, SUBLANES)
                acc = acc + dw_ref[k:k + 1, lanes] * xs_ref[s, r0 + SUBLANES * a:r0 + SUBLANES * a + CONV_ROWS, :]
            h_ref[r0:r0 + CONV_ROWS, lanes] = acc + dwb_ref[:, lanes]
        return carry

    lax.fori_loop(0, d // CONV_LANES, lane_block, 0)

    def row_block(rb, carry):
        r0 = pl.multiple_of(rb * CONV_ROWS, CONV_ROWS)
        h = h_ref[pl.ds(r0, CONV_ROWS), :]
        mu = jnp.mean(h, axis=-1, keepdims=True)
        hc = h - mu
        var = jnp.mean(hc * hc, axis=-1, keepdims=True)
        y = hc * lax.rsqrt(var + LN_EPS) * lng_ref[...] + lnb_ref[...]
        y_ref[pl.ds(r0, CONV_ROWS), :] = (y * jax.nn.sigmoid(y)).astype(BF16)
        return carry

    lax.fori_loop(0, tm // CONV_ROWS, row_block, 0)
    o_ref[...] = x_ref[...] + _dot(y_ref[...], wo_ref[...]) + bo_ref[...]


def _conf_out(glu, x2, dw, dwb, ln_g, ln_b, w_out, b_out, *, tm, seq_len):
    m, d = x2.shape
    width = dw.shape[0]
    assert width - 1 <= CONV_HALO and tm % CONV_HALO == 0 and tm % CONV_ROWS == 0
    assert d % CONV_LANES == 0
    hb = tm // CONV_HALO
    row = lambda a: a.reshape(1, d)
    kern = functools.partial(_conf_out_kernel, tm=tm, seq_len=seq_len, width=width)
    return pl.pallas_call(
        kern,
        grid=(m // tm,),
        in_specs=[
            pl.BlockSpec((tm, d), lambda i: (i, 0)),
            pl.BlockSpec((CONV_HALO, d), lambda i: (jnp.maximum(i * hb - 1, 0), 0)),
            pl.BlockSpec((tm, d), lambda i: (i, 0)),
            pl.BlockSpec((width, d), lambda i: (0, 0)),
            pl.BlockSpec((1, d), lambda i: (0, 0)),
            pl.BlockSpec((1, d), lambda i: (0, 0)),
            pl.BlockSpec((1, d), lambda i: (0, 0)),
            pl.BlockSpec((d, d), lambda i: (0, 0)),
            pl.BlockSpec((1, d), lambda i: (0, 0)),
        ],
        out_specs=pl.BlockSpec((tm, d), lambda i: (i, 0)),
        out_shape=jax.ShapeDtypeStruct((m, d), F32),
        scratch_shapes=[pltpu.VMEM((CONV_HALO + tm, d), F32),
                        pltpu.VMEM((SUBLANES, CONV_HALO + tm, CONV_LANES), F32),
                        pltpu.VMEM((tm, d), F32),
                        pltpu.VMEM((tm, d), BF16)],
        compiler_params=_params("arbitrary"),
        name="conf_out",
    )(glu, glu, x2, dw, row(dwb), row(ln_g), row(ln_b), w_out, row(b_out))


def _ffn_kernel(x_ref, ng_ref, wg_ref, wu_ref, dw_ref, dwb_ref, wd_ref, fg_ref, o_ref,
                hn_ref, gbuf_ref, carry_ref, *, tm, seq_len, final_norm):
    i = pl.program_id(0)
    j = pl.program_id(1)

    @pl.when(j == 0)
    def _():
        x = x_ref[...]
        hn_ref[...] = _rms(x, ng_ref[...]).astype(BF16)
        o_ref[...] = x

    hn = hn_ref[...]
    tf = wg_ref.shape[1]
    seq_start = (i * tm) % seq_len == 0

    @pl.when(seq_start)
    def _():
        gbuf_ref[0:FFN_HALO, :] = jnp.zeros((FFN_HALO, tf), F32)

    @pl.when(jnp.logical_not(seq_start))
    def _():
        gbuf_ref[0:FFN_HALO, :] = carry_ref[j]

    down = None
    for c in range(tf // V7X_MXU_DIM):
        cols = slice(c * V7X_MXU_DIM, (c + 1) * V7X_MXU_DIM)
        g_lin = _dot(hn, wg_ref[:, cols])
        up = _dot(hn, wu_ref[:, cols])
        gbuf_ref[FFN_HALO:, cols] = g_lin
        carry_ref[j, :, cols] = g_lin[tm - FFN_HALO:, :]
        g = (dw_ref[2:3, cols] * g_lin
             + dw_ref[1:2, cols] * gbuf_ref[FFN_HALO - 1:FFN_HALO - 1 + tm, cols]
             + dw_ref[0:1, cols] * gbuf_ref[FFN_HALO - 2:FFN_HALO - 2 + tm, cols]
             + dwb_ref[:, cols])
        act = (g * jax.nn.sigmoid(g) * up).astype(BF16)
        part = _dot(act, wd_ref[cols, :])
        down = part if down is None else down + part
    o_ref[...] += down

    if final_norm:
        @pl.when(j == pl.num_programs(1) - 1)
        def _():
            o_ref[...] = _rms(o_ref[...], fg_ref[...])


def _ffn(x2, norm_g, w_gate, w_up, dw, dwb, w_down, final_g, *, tm, tf, seq_len):
    m, d = x2.shape
    fp = w_gate.shape[1]
    assert dw.shape[0] == 3 and fp % tf == 0
    nj = fp // tf
    final_norm = final_g is not None
    fg = (final_g if final_norm else jnp.ones((d,), F32)).reshape(1, d)
    kern = functools.partial(_ffn_kernel, tm=tm, seq_len=seq_len, final_norm=final_norm)
    return pl.pallas_call(
        kern,
        grid=(m // tm, nj),
        in_specs=[
            pl.BlockSpec((tm, d), lambda i, j: (i, 0)),
            pl.BlockSpec((1, d), lambda i, j: (0, 0)),
            pl.BlockSpec((d, tf), lambda i, j: (0, j)),
            pl.BlockSpec((d, tf), lambda i, j: (0, j)),
            pl.BlockSpec((3, tf), lambda i, j: (0, j)),
            pl.BlockSpec((1, tf), lambda i, j: (0, j)),
            pl.BlockSpec((tf, d), lambda i, j: (j, 0)),
            pl.BlockSpec((1, d), lambda i, j: (0, 0)),
        ],
        out_specs=pl.BlockSpec((tm, d), lambda i, j: (i, 0)),
        out_shape=jax.ShapeDtypeStruct((m, d), F32),
        scratch_shapes=[
            pltpu.VMEM((tm, d), BF16),
            pltpu.VMEM((FFN_HALO + tm, tf), F32),
            pltpu.VMEM((nj, FFN_HALO, tf), F32),
        ],
        compiler_params=_params("arbitrary", "arbitrary"),
        name="ffn",
    )(x2, norm_g.reshape(1, d), w_gate, w_up, dw, dwb.reshape(1, fp), w_down, fg)


def _qkv_kernel(x_ref, ng_ref, pos_ref, invf_ref, w_ref, o_ref, hn_ref, cos_ref, sin_ref,
                *, tiles_per_kind, head_dim, half, q_scale):
    j = pl.program_id(1)
    tn = o_ref.shape[1]
    pair = head_dim // 2

    @pl.when(j == 0)
    def _():
        hn_ref[...] = _rms(x_ref[...], ng_ref[...]).astype(BF16)
        ang = pos_ref[...].astype(F32) * invf_ref[...]
        lane = lax.broadcasted_iota(jnp.int32, ang.shape, 1)
        cos = jnp.cos(ang)
        sin = jnp.where(lane < half, -jnp.sin(ang), jnp.sin(ang))
        cos_ref[0] = cos * q_scale
        sin_ref[0] = sin * q_scale
        cos_ref[1] = cos
        sin_ref[1] = sin
        cos_ref[2] = jnp.ones_like(cos)
        sin_ref[2] = jnp.zeros_like(sin)

    kind = jnp.minimum(j // tiles_per_kind, 2)
    cos = cos_ref[kind]
    sin = sin_ref[kind]
    hn = hn_ref[...]
    for c in range(tn // V7X_MXU_DIM):
        acc = _dot(hn, w_ref[:, c * V7X_MXU_DIM:(c + 1) * V7X_MXU_DIM])
        for h in range(V7X_MXU_DIM // head_dim):
            seg = acc[:, h * head_dim:(h + 1) * head_dim]
            col = c * V7X_MXU_DIM + h * head_dim
            o_ref[:, col:col + head_dim] = (seg * cos + pltpu.roll(seg, pair, 1) * sin).astype(BF16)


def _rope_layout(head_dim):
    rot = head_dim // 4
    half = rot // 2
    pair = head_dim // 2
    blocks = [(0, half), (pair, pair + half), (rot, pair), (half, rot), (pair + half, head_dim)]
    inv_freq = ROPE_THETA ** (-jnp.arange(half, dtype=F32) / half)
    invf = jnp.zeros((head_dim,), F32).at[:half].set(inv_freq).at[pair:pair + half].set(inv_freq)
    return blocks, invf.reshape(1, head_dim), half


def _qkv(x2, norm_g, positions, w_qkv, invf, *, tm, tn, head_dim, half, qk_width):
    m, d = x2.shape
    n = w_qkv.shape[1]
    assert head_dim == V7X_LANES and qk_width % tn == 0 and n % tn == 0 and tn % V7X_MXU_DIM == 0
    kern = functools.partial(_qkv_kernel, tiles_per_kind=qk_width // tn, head_dim=head_dim, half=half,
                             q_scale=head_dim ** -0.5 * math.log2(math.e))
    return pl.pallas_call(
        kern,
        grid=(m // tm, n // tn),
        in_specs=[
            pl.BlockSpec((tm, d), lambda i, j: (i, 0)),
            pl.BlockSpec((1, d), lambda i, j: (0, 0)),
            pl.BlockSpec((tm, 1), lambda i, j: (i, 0)),
            pl.BlockSpec((1, head_dim), lambda i, j: (0, 0)),
            pl.BlockSpec((d, tn), lambda i, j: (0, j)),
        ],
        out_specs=pl.BlockSpec((tm, tn), lambda i, j: (i, j)),
        out_shape=jax.ShapeDtypeStruct((m, n), BF16),
        scratch_shapes=[pltpu.VMEM((tm, d), BF16)] + [pltpu.VMEM((3, tm, head_dim), F32)] * 2,
        compiler_params=_params("arbitrary", "arbitrary"),
        name="qkv",
    )(x2, norm_g.reshape(1, d), positions.reshape(m, 1), invf, w_qkv)


def _attn_kernel(q_ref, k_ref, v_ref, lq1_ref, lk1_ref, lq2_ref, lk2_ref, sg_ref, o_ref,
                 m_ref, l_ref, acc_ref, *, tq, head_dim, lam_init):
    qi = pl.program_id(2)
    vd = 2 * head_dim
    m_ref[...] = jnp.full(m_ref.shape, MASK_VALUE, F32)
    l_ref[...] = jnp.zeros(l_ref.shape, F32)
    acc_ref[...] = jnp.zeros(acc_ref.shape, F32)

    def update(k0, kw, r0, nr, masked):
        rows = slice(r0, r0 + nr)
        v = v_ref[pl.ds(k0, kw), :]
        for idx in range(2):
            cols = slice(idx * head_dim, (idx + 1) * head_dim)
            s = lax.dot_general(q_ref[rows, cols], k_ref[pl.ds(k0, kw), cols],
                                (((1,), (1,)), ((), ())), preferred_element_type=F32)
            if masked:
                row = lax.broadcasted_iota(jnp.int32, s.shape, 0)
                col = lax.broadcasted_iota(jnp.int32, s.shape, 1)
                s = jnp.where(col <= row, s, MASK_VALUE)
            m_prev = m_ref[idx, rows, :]
            m_new = jnp.maximum(m_prev, jnp.max(s, axis=1, keepdims=True))
            alpha = jnp.exp2(m_prev - m_new)
            ps = [jnp.exp2(s[:, c * V7X_LANES:(c + 1) * V7X_LANES] - m_new)
                  for c in range(kw // V7X_LANES)]
            l_ref[idx, rows, :] = alpha * l_ref[idx, rows, :] + functools.reduce(jnp.add, ps)
            pv = _dot(jnp.concatenate(ps, axis=1).astype(BF16), v)
            alpha_v = jnp.concatenate([alpha] * (vd // V7X_LANES), axis=1)
            acc_ref[idx, rows, :] = alpha_v * acc_ref[idx, rows, :] + pv
            m_ref[idx, rows, :] = m_new

    def body(kj, carry):
        update(pl.multiple_of(kj * tq, tq), tq, 0, tq, False)
        return carry

    lax.fori_loop(0, qi, body, 0)
    half = tq // 2
    d0 = pl.multiple_of(qi * tq, tq)
    update(d0, half, 0, tq, True)
    update(d0 + half, half, half, half, True)

    lam = (jnp.exp(jnp.sum(lq1_ref[...] * lk1_ref[...], keepdims=True))
           - jnp.exp(jnp.sum(lq2_ref[...] * lk2_ref[...], keepdims=True)) + lam_init)
    l0 = jnp.sum(l_ref[0], axis=1, keepdims=True)
    l1 = jnp.sum(l_ref[1], axis=1, keepdims=True)
    o = acc_ref[0] / l0 - lam * (acc_ref[1] / l1)
    o_ref[...] = (_rms(o, sg_ref[...]) * (1.0 - lam_init)).astype(BF16)


def _attn(qkv, lq1, lk1, lq2, lk2, subln_g, *, batch, seq_len, heads, head_dim, tq, lam_init):
    m = qkv.shape[0]
    vd = 2 * head_dim
    nq = seq_len // tq
    k_col0 = heads
    v_col0 = 2 * heads
    vec = lambda a: a.reshape(1, head_dim)
    kern = functools.partial(_attn_kernel, tq=tq, head_dim=head_dim, lam_init=lam_init)
    small = pl.BlockSpec((1, head_dim), lambda b, h, q: (0, 0))
    return pl.pallas_call(
        kern,
        grid=(batch, heads, nq),
        in_specs=[
            pl.BlockSpec((tq, vd), lambda b, h, q: (b * nq + q, h)),
            pl.BlockSpec((seq_len, vd), lambda b, h, q: (b, k_col0 + h)),
            pl.BlockSpec((seq_len, vd), lambda b, h, q: (b, v_col0 + h)),
            small, small, small, small,
            pl.BlockSpec((1, vd), lambda b, h, q: (0, 0)),
        ],
        out_specs=pl.BlockSpec((tq, vd), lambda b, h, q: (b * nq + q, h)),
        out_shape=jax.ShapeDtypeStruct((m, heads * vd), BF16),
        scratch_shapes=[
            pltpu.VMEM((2, tq, V7X_LANES), F32),
            pltpu.VMEM((2, tq, V7X_LANES), F32),
            pltpu.VMEM((2, tq, vd), F32),
        ],
        compiler_params=_params("arbitrary", "arbitrary", "arbitrary"),
        name="diff_attn",
    )(qkv, qkv, qkv, vec(lq1), vec(lk1), vec(lq2), vec(lk2), subln_g.reshape(1, vd))


def _proj_res_kernel(a_ref, w_ref, r_ref, o_ref):
    o_ref[...] = r_ref[...] + _dot(a_ref[...], w_ref[...])


def _proj_res(a, w, res, *, tm):
    m, k = a.shape
    n = w.shape[1]
    return pl.pallas_call(
        _proj_res_kernel,
        grid=(m // tm,),
        in_specs=[
            pl.BlockSpec((tm, k), lambda i: (i, 0)),
            pl.BlockSpec((k, n), lambda i: (0, 0)),
            pl.BlockSpec((tm, n), lambda i: (i, 0)),
        ],
        out_specs=pl.BlockSpec((tm, n), lambda i: (i, 0)),
        out_shape=jax.ShapeDtypeStruct((m, n), F32),
        compiler_params=_params("arbitrary"),
        name="attn_out",
    )(a, w, res)


def _pad_cols(a, n):
    return a if a.shape[-1] == n else jnp.pad(a, [(0, 0)] * (a.ndim - 1) + [(0, n - a.shape[-1])])


def kernel(x, positions, norm_mix, norm_ffn, norm_final, conv_w_in, conv_b_in, conv_dw, conv_dw_b, conv_ln_g, conv_ln_b, conv_w_out, conv_b_out, attn_w_qkv, attn_lq1, attn_lk1, attn_lq2, attn_lk2, attn_subln_g, attn_w_o, ffn_w_gate, ffn_w_up, ffn_dw, ffn_dw_b, ffn_w_down):
    batch, seq_len, d = x.shape
    depth = norm_mix.shape[0]
    m = batch * seq_len
    d_ff = ffn_w_gate.shape[2]
    head_dim = attn_lq1.shape[1]
    heads = d // (2 * head_dim)
    qk_width = 2 * heads * head_dim

    tm = _pick(seq_len, (1024, 512, 256, 128, 64, 32))
    tm_proj = _pick(seq_len, (512, 256, 128, 64, 32))
    tm_conv = _pick(seq_len, (256, 128, 64, 32))
    tn = _pick(d, (512, 256))
    tq = _pick(seq_len, (512, 256))
    tf = 2 * V7X_MXU_DIM
    f_pad = -(-d_ff // tf) * tf

    x2 = x.reshape(m, d)
    for i in range(depth):
        j = i // N_MIXERS
        if i % N_MIXERS == 0:
            glu = _conf_in(x2, norm_mix[i], conv_w_in[j].astype(BF16), conv_b_in[j], tm=tm, tn=tn)
            x2 = _conf_out(glu, x2, conv_dw[j], conv_dw_b[j], conv_ln_g[j], conv_ln_b[j],
                           conv_w_out[j].astype(BF16), conv_b_out[j], tm=tm_conv, seq_len=seq_len)
        else:
            lam_init = 0.8 - 0.6 * math.exp(-0.3 * i)
            blocks, invf, half = _rope_layout(head_dim)
            w = attn_w_qkv[j]
            w_qk = w[:, :2 * qk_width].reshape(d, -1, head_dim)
            w_qk = jnp.concatenate([w_qk[:, :, a:b] for a, b in blocks], axis=2).reshape(d, 2 * qk_width)
            w_qkv = jnp.concatenate([w_qk, w[:, 2 * qk_width:]], axis=1).astype(BF16)
            qkv = _qkv(x2, norm_mix[i], positions, w_qkv, invf,
                       tm=tm, tn=tn, head_dim=head_dim, half=half, qk_width=qk_width)
            o = _attn(qkv, attn_lq1[j], attn_lk1[j], attn_lq2[j], attn_lk2[j], attn_subln_g[j],
                      batch=batch, seq_len=seq_len, heads=heads, head_dim=head_dim, tq=tq,
                      lam_init=lam_init)
            x2 = _proj_res(o, attn_w_o[j].astype(BF16), x2, tm=tm_proj)
        final_g = norm_final if i == depth - 1 else None
        x2 = _ffn(x2, norm_ffn[i],
                  _pad_cols(ffn_w_gate[i], f_pad).astype(BF16),
                  _pad_cols(ffn_w_up[i], f_pad).astype(BF16),
                  _pad_cols(ffn_dw[i], f_pad), _pad_cols(ffn_dw_b[i], f_pad),
                  jnp.pad(ffn_w_down[i], ((0, f_pad - d_ff), (0, 0))).astype(BF16),
                  final_g, tm=tm, tf=tf, seq_len=seq_len)
    if depth == 0:
        raise ValueError("depth must be positive")
    return x2.reshape(batch, seq_len, d)
```

```python
import functools
import math

import jax
import jax.numpy as jnp
from jax import lax
from jax.experimental import pallas as pl
from jax.experimental.pallas import tpu as pltpu

RMS_EPS = 1e-6
LN_EPS = 1e-5
ROPE_THETA = 500000.0
N_MIXERS = 2

V7X_LANES = 128
SUBLANES = 8
V7X_MXU_DIM = 256
V7X_VMEM_LIMIT_BYTES = 60 * 1024 * 1024

F32 = jnp.float32
BF16 = jnp.bfloat16
MASK_VALUE = -1e30

CONV_HALO = 32
CONV_ROWS = 64
CONV_LANES = 256
LN_ROWS = 32
LN_UNROLL = 2
FFN_HALO = 8
CAST_BLOCK_ELEMS = 1024 * 1024


def _pick(n, candidates):
    for c in candidates:
        if n % c == 0:
            return c
    raise ValueError(f"no tile in {candidates} divides {n}")


def _params(*sem):
    return pltpu.CompilerParams(dimension_semantics=sem,
                                vmem_limit_bytes=V7X_VMEM_LIMIT_BYTES)


def _rms(x, g):
    ms = jnp.mean(x * x, axis=-1, keepdims=True)
    return x * lax.rsqrt(ms + RMS_EPS) * g


def _dot(a, b):
    return jnp.dot(a, b, preferred_element_type=F32)


def _conf_in_kernel(x_ref, ng_ref, wa_ref, wb_ref, ba_ref, bb_ref, o_ref, hn_ref):
    @pl.when(pl.program_id(1) == 0)
    def _():
        hn_ref[...] = _rms(x_ref[...], ng_ref[...]).astype(BF16)

    hn = hn_ref[...]
    for c in range(o_ref.shape[1] // V7X_MXU_DIM):
        cols = slice(c * V7X_MXU_DIM, (c + 1) * V7X_MXU_DIM)
        a = _dot(hn, wa_ref[:, cols]) + ba_ref[:, cols]
        gate = _dot(hn, wb_ref[:, cols]) + bb_ref[:, cols]
        o_ref[:, cols] = a * jax.nn.sigmoid(gate)


def _conf_in(x2, norm_g, w_in, b_in, *, tm, tn):
    m, d = x2.shape
    nj = d // tn
    return pl.pallas_call(
        _conf_in_kernel,
        grid=(m // tm, nj),
        in_specs=[
            pl.BlockSpec((tm, d), lambda i, j: (i, 0)),
            pl.BlockSpec((1, d), lambda i, j: (0, 0)),
            pl.BlockSpec((d, tn), lambda i, j: (0, j)),
            pl.BlockSpec((d, tn), lambda i, j: (0, j + nj)),
            pl.BlockSpec((1, tn), lambda i, j: (0, j)),
            pl.BlockSpec((1, tn), lambda i, j: (0, j + nj)),
        ],
        out_specs=pl.BlockSpec((tm, tn), lambda i, j: (i, j)),
        out_shape=jax.ShapeDtypeStruct((m, d), F32),
        scratch_shapes=[pltpu.VMEM((tm, d), BF16)],
        compiler_params=_params("arbitrary", "arbitrary"),
        name="conf_in",
    )(x2, norm_g.reshape(1, d), w_in, w_in, b_in.reshape(1, 2 * d), b_in.reshape(1, 2 * d))


def _conf_out_kernel(glu_ref, halo_ref, x_ref, dw_ref, dwb_ref, lng_ref, lnb_ref,
                     wo_ref, bo_ref, o_ref, cbuf_ref, xs_ref, wb_ref, h_ref, y_ref, *, tm, seq_len, width):
    d = glu_ref.shape[1]
    seq_start = (pl.program_id(0) * tm) % seq_len == 0

    @pl.when(pl.program_id(0) == 0)
    def _():
        for k in range(width):
            wb_ref[k] = jnp.broadcast_to(dw_ref[k:k + 1, :], (SUBLANES, d))

    @pl.when(seq_start)
    def _():
        cbuf_ref[0:CONV_HALO, :] = jnp.zeros((CONV_HALO, d), F32)

    @pl.when(jnp.logical_not(seq_start))
    def _():
        cbuf_ref[0:CONV_HALO, :] = halo_ref[...]

    cbuf_ref[CONV_HALO:, :] = glu_ref[...]

    base = CONV_HALO - (width - 1)
    def lane_block(c, carry):
        lanes = pl.ds(pl.multiple_of(c * CONV_LANES, CONV_LANES), CONV_LANES)
        for s in range(min(SUBLANES, width)):
            span = tm + SUBLANES * ((width - 1 - s) // SUBLANES)
            xs_ref[s, 0:span, :] = cbuf_ref[base + s:base + s + span, lanes]
        def conv_rows(rb, carry2):
            r0 = pl.multiple_of(rb * CONV_ROWS, CONV_ROWS)
            groups = CONV_ROWS // SUBLANES
            accs = [jnp.zeros((SUBLANES, CONV_LANES), F32)] * groups
            for k in range(width):
                a, s = divmod(k, SUBLANES)
                w8 = wb_ref[k, :, lanes]
                for g in range(groups):
                    accs[g] = accs[g] + w8 * xs_ref[s, pl.ds(r0 + SUBLANES * (a + g), SUBLANES), :]
            bias = dwb_ref[:, lanes]
            for g in range(groups):
                h_ref[pl.ds(r0 + SUBLANES * g, SUBLANES), lanes] = accs[g] + bias
            return carry2

        lax.fori_loop(0, tm // CONV_ROWS, conv_rows, 0)
        return carry

    lax.fori_loop(0, d // CONV_LANES, lane_block, 0)

    def row_block(rb, carry):
        r0 = pl.multiple_of(rb * LN_ROWS, LN_ROWS)
        h = h_ref[pl.ds(r0, LN_ROWS), :]
        mu = jnp.mean(h, axis=-1, keepdims=True)
        hc = h - mu
        var = jnp.mean(hc * hc, axis=-1, keepdims=True)
        y = hc * lax.rsqrt(var + LN_EPS) * lng_ref[...] + lnb_ref[...]
        y_ref[pl.ds(r0, LN_ROWS), :] = (y * jax.nn.sigmoid(y)).astype(BF16)
        return carry

    trips = tm // LN_ROWS
    lax.fori_loop(0, trips, row_block, 0, unroll=math.gcd(trips, LN_UNROLL))
    o_ref[...] = x_ref[...] + _dot(y_ref[...], wo_ref[...]) + bo_ref[...]


def _conf_out(glu, x2, dw, dwb, ln_g, ln_b, w_out, b_out, *, tm, seq_len):
    m, d = x2.shape
    width = dw.shape[0]
    assert width - 1 <= CONV_HALO and tm % CONV_HALO == 0 and tm % CONV_ROWS == 0 and tm % LN_ROWS == 0
    assert d % CONV_LANES == 0
    hb = tm // CONV_HALO
    row = lambda a: a.reshape(1, d)
    kern = functools.partial(_conf_out_kernel, tm=tm, seq_len=seq_len, width=width)
    return pl.pallas_call(
        kern,
        grid=(m // tm,),
        in_specs=[
            pl.BlockSpec((tm, d), lambda i: (i, 0)),
            pl.BlockSpec((CONV_HALO, d), lambda i: (jnp.maximum(i * hb - 1, 0), 0)),
            pl.BlockSpec((tm, d), lambda i: (i, 0)),
            pl.BlockSpec((width, d), lambda i: (0, 0)),
            pl.BlockSpec((1, d), lambda i: (0, 0)),
            pl.BlockSpec((1, d), lambda i: (0, 0)),
            pl.BlockSpec((1, d), lambda i: (0, 0)),
            pl.BlockSpec((d, d), lambda i: (0, 0)),
            pl.BlockSpec((1, d), lambda i: (0, 0)),
        ],
        out_specs=pl.BlockSpec((tm, d), lambda i: (i, 0)),
        out_shape=jax.ShapeDtypeStruct((m, d), F32),
        scratch_shapes=[pltpu.VMEM((CONV_HALO + tm, d), F32),
                        pltpu.VMEM((SUBLANES, CONV_HALO + tm, CONV_LANES), F32),
                        pltpu.VMEM((width, SUBLANES, d), F32),
                        pltpu.VMEM((tm, d), F32),
                        pltpu.VMEM((tm, d), BF16)],
        compiler_params=_params("arbitrary"),
        name="conf_out",
    )(glu, glu, x2, dw, row(dwb), row(ln_g), row(ln_b), w_out, row(b_out))


def _ffn_kernel(x_ref, ng_ref, wg_ref, wu_ref, dw_ref, dwb_ref, wd_ref, fg_ref, o_ref,
                hn_ref, gbuf_ref, carry_ref, *, tm, seq_len, final_norm):
    i = pl.program_id(0)
    j = pl.program_id(1)

    @pl.when(j == 0)
    def _():
        x = x_ref[...]
        hn_ref[...] = _rms(x, ng_ref[...]).astype(BF16)
        o_ref[...] = x

    hn = hn_ref[...]
    tf = wg_ref.shape[1]
    seq_start = (i * tm) % seq_len == 0

    @pl.when(seq_start)
    def _():
        gbuf_ref[0:FFN_HALO, :] = jnp.zeros((FFN_HALO, tf), F32)

    @pl.when(jnp.logical_not(seq_start))
    def _():
        gbuf_ref[0:FFN_HALO, :] = carry_ref[j]

    down = None
    for c in range(tf // V7X_MXU_DIM):
        cols = slice(c * V7X_MXU_DIM, (c + 1) * V7X_MXU_DIM)
        g_lin = _dot(hn, wg_ref[:, cols])
        up = _dot(hn, wu_ref[:, cols])
        gbuf_ref[FFN_HALO:, cols] = g_lin
        carry_ref[j, :, cols] = g_lin[tm - FFN_HALO:, :]
        g = (dw_ref[2:3, cols] * g_lin
             + dw_ref[1:2, cols] * gbuf_ref[FFN_HALO - 1:FFN_HALO - 1 + tm, cols]
             + dw_ref[0:1, cols] * gbuf_ref[FFN_HALO - 2:FFN_HALO - 2 + tm, cols]
             + dwb_ref[:, cols])
        act = (g * jax.nn.sigmoid(g) * up).astype(BF16)
        part = _dot(act, wd_ref[cols, :])
        down = part if down is None else down + part
    o_ref[...] += down

    if final_norm:
        @pl.when(j == pl.num_programs(1) - 1)
        def _():
            o_ref[...] = _rms(o_ref[...], fg_ref[...])


def _ffn(x2, norm_g, w_gate, w_up, dw, dwb, w_down, final_g, *, tm, tf, seq_len):
    m, d = x2.shape
    fp = w_gate.shape[1]
    assert dw.shape[0] == 3 and fp % tf == 0
    nj = fp // tf
    final_norm = final_g is not None
    fg = (final_g if final_norm else jnp.ones((d,), F32)).reshape(1, d)
    kern = functools.partial(_ffn_kernel, tm=tm, seq_len=seq_len, final_norm=final_norm)
    return pl.pallas_call(
        kern,
        grid=(m // tm, nj),
        in_specs=[
            pl.BlockSpec((tm, d), lambda i, j: (i, 0)),
            pl.BlockSpec((1, d), lambda i, j: (0, 0)),
            pl.BlockSpec((d, tf), lambda i, j: (0, j)),
            pl.BlockSpec((d, tf), lambda i, j: (0, j)),
            pl.BlockSpec((3, tf), lambda i, j: (0, j)),
            pl.BlockSpec((1, tf), lambda i, j: (0, j)),
            pl.BlockSpec((tf, d), lambda i, j: (j, 0)),
            pl.BlockSpec((1, d), lambda i, j: (0, 0)),
        ],
        out_specs=pl.BlockSpec((tm, d), lambda i, j: (i, 0)),
        out_shape=jax.ShapeDtypeStruct((m, d), F32),
        scratch_shapes=[
            pltpu.VMEM((tm, d), BF16),
            pltpu.VMEM((FFN_HALO + tm, tf), F32),
            pltpu.VMEM((nj, FFN_HALO, tf), F32),
        ],
        compiler_params=_params("arbitrary", "arbitrary"),
        name="ffn",
    )(x2, norm_g.reshape(1, d), w_gate, w_up, dw, dwb.reshape(1, fp), w_down, fg)


def _qkv_kernel(x_ref, ng_ref, pos_ref, invf_ref, w_ref, o_ref, hn_ref, cos_ref, sin_ref,
                *, tiles_per_kind, head_dim, half, q_scale):
    j = pl.program_id(1)
    tn = o_ref.shape[1]
    pair = head_dim // 2

    @pl.when(j == 0)
    def _():
        hn_ref[...] = _rms(x_ref[...], ng_ref[...]).astype(BF16)
        ang = pos_ref[...].astype(F32) * invf_ref[...]
        lane = lax.broadcasted_iota(jnp.int32, ang.shape, 1)
        cos = jnp.cos(ang)
        sin = jnp.where(lane < half, -jnp.sin(ang), jnp.sin(ang))
        cos_ref[0] = cos * q_scale
        sin_ref[0] = sin * q_scale
        cos_ref[1] = cos
        sin_ref[1] = sin
        cos_ref[2] = jnp.ones_like(cos)
        sin_ref[2] = jnp.zeros_like(sin)

    kind = jnp.minimum(j // tiles_per_kind, 2)
    cos = cos_ref[kind]
    sin = sin_ref[kind]
    hn = hn_ref[...]
    for c in range(tn // V7X_MXU_DIM):
        acc = _dot(hn, w_ref[:, c * V7X_MXU_DIM:(c + 1) * V7X_MXU_DIM])
        for h in range(V7X_MXU_DIM // head_dim):
            seg = acc[:, h * head_dim:(h + 1) * head_dim]
            col = c * V7X_MXU_DIM + h * head_dim
            o_ref[:, col:col + head_dim] = (seg * cos + pltpu.roll(seg, pair, 1) * sin).astype(BF16)


def _rope_layout(head_dim):
    rot = head_dim // 4
    half = rot // 2
    pair = head_dim // 2
    inv_freq = ROPE_THETA ** (-jnp.arange(half, dtype=F32) / half)
    invf = jnp.zeros((head_dim,), F32).at[:half].set(inv_freq).at[pair:pair + half].set(inv_freq)
    return (half, pair, half), invf.reshape(1, head_dim), half


def _qkv(x2, norm_g, positions, w_qkv, invf, *, tm, tn, head_dim, half, qk_width):
    m, d = x2.shape
    n = w_qkv.shape[1]
    assert head_dim == V7X_LANES and qk_width % tn == 0 and n % tn == 0 and tn % V7X_MXU_DIM == 0
    kern = functools.partial(_qkv_kernel, tiles_per_kind=qk_width // tn, head_dim=head_dim, half=half,
                             q_scale=head_dim ** -0.5 * math.log2(math.e))
    return pl.pallas_call(
        kern,
        grid=(m // tm, n // tn),
        in_specs=[
            pl.BlockSpec((tm, d), lambda i, j: (i, 0)),
            pl.BlockSpec((1, d), lambda i, j: (0, 0)),
            pl.BlockSpec((tm, 1), lambda i, j: (i, 0)),
            pl.BlockSpec((1, head_dim), lambda i, j: (0, 0)),
            pl.BlockSpec((d, tn), lambda i, j: (0, j)),
        ],
        out_specs=pl.BlockSpec((tm, tn), lambda i, j: (i, j)),
        out_shape=jax.ShapeDtypeStruct((m, n), BF16),
        scratch_shapes=[pltpu.VMEM((tm, d), BF16)] + [pltpu.VMEM((3, tm, head_dim), F32)] * 2,
        compiler_params=_params("arbitrary", "arbitrary"),
        name="qkv",
    )(x2, norm_g.reshape(1, d), positions.reshape(m, 1), invf, w_qkv)


def _attn_kernel(q_ref, k_ref, v_ref, lq1_ref, lk1_ref, lq2_ref, lk2_ref, sg_ref, o_ref,
                 m_ref, l_ref, acc_ref, sa_ref, sb_ref, *, tq, head_dim, lam_init):
    qi = pl.program_id(2)
    vd = 2 * head_dim
    m_ref[...] = jnp.full(m_ref.shape, MASK_VALUE, F32)
    l_ref[...] = jnp.zeros(l_ref.shape, F32)
    acc_ref[...] = jnp.zeros(acc_ref.shape, F32)

    def scores(kj, s_ref):
        k0 = pl.multiple_of(kj * tq, tq)
        for idx in range(2):
            cols = slice(idx * head_dim, (idx + 1) * head_dim)
            s_ref[idx] = lax.dot_general(q_ref[:, cols], k_ref[pl.ds(k0, tq), cols],
                                         (((1,), (1,)), ((), ())), preferred_element_type=F32)

    def fold(kj, s_ref, masked):
        v = v_ref[pl.ds(pl.multiple_of(kj * tq, tq), tq), :]
        for idx in range(2):
            s = s_ref[idx]
            if masked:
                row = lax.broadcasted_iota(jnp.int32, s.shape, 0)
                col = lax.broadcasted_iota(jnp.int32, s.shape, 1)
                s = jnp.where(col <= row, s, MASK_VALUE)
            m_prev = m_ref[idx]
            m_new = jnp.maximum(m_prev, jnp.max(s, axis=1, keepdims=True))
            alpha = jnp.exp2(m_prev - m_new)
            ps = [jnp.exp2(s[:, c * V7X_LANES:(c + 1) * V7X_LANES] - m_new)
                  for c in range(tq // V7X_LANES)]
            l_ref[idx] = alpha * l_ref[idx] + functools.reduce(jnp.add, ps)
            pv = _dot(jnp.concatenate(ps, axis=1).astype(BF16), v)
            alpha_v = jnp.concatenate([alpha] * (vd // V7X_LANES), axis=1)
            acc_ref[idx] = alpha_v * acc_ref[idx] + pv
            m_ref[idx] = m_new

    scores(0, sa_ref)

    def pair(p, carry):
        scores(2 * p + 1, sb_ref)
        fold(2 * p, sa_ref, False)
        scores(2 * p + 2, sa_ref)
        fold(2 * p + 1, sb_ref, False)
        return carry

    lax.fori_loop(0, lax.shift_right_logical(qi, jnp.int32(1)), pair, 0)

    @pl.when((qi & 1) == 0)
    def _():
        fold(qi, sa_ref, True)

    @pl.when((qi & 1) == 1)
    def _():
        scores(qi, sb_ref)
        fold(qi - 1, sa_ref, False)
        fold(qi, sb_ref, True)

    lam = (jnp.exp(jnp.sum(lq1_ref[...] * lk1_ref[...], keepdims=True))
           - jnp.exp(jnp.sum(lq2_ref[...] * lk2_ref[...], keepdims=True)) + lam_init)
    l0 = jnp.sum(l_ref[0], axis=1, keepdims=True)
    l1 = jnp.sum(l_ref[1], axis=1, keepdims=True)
    o = acc_ref[0] / l0 - lam * (acc_ref[1] / l1)
    o_ref[...] = (_rms(o, sg_ref[...]) * (1.0 - lam_init)).astype(BF16)


def _attn(qkv, lq1, lk1, lq2, lk2, subln_g, *, batch, seq_len, heads, head_dim, tq, lam_init):
    m = qkv.shape[0]
    vd = 2 * head_dim
    nq = seq_len // tq
    k_col0 = heads
    v_col0 = 2 * heads
    vec = lambda a: a.reshape(1, head_dim)
    kern = functools.partial(_attn_kernel, tq=tq, head_dim=head_dim, lam_init=lam_init)
    small = pl.BlockSpec((1, head_dim), lambda b, h, q: (0, 0))
    return pl.pallas_call(
        kern,
        grid=(batch, heads, nq),
        in_specs=[
            pl.BlockSpec((tq, vd), lambda b, h, q: (b * nq + q, h)),
            pl.BlockSpec((seq_len, vd), lambda b, h, q: (b, k_col0 + h)),
            pl.BlockSpec((seq_len, vd), lambda b, h, q: (b, v_col0 + h)),
            small, small, small, small,
            pl.BlockSpec((1, vd), lambda b, h, q: (0, 0)),
        ],
        out_specs=pl.BlockSpec((tq, vd), lambda b, h, q: (b * nq + q, h)),
        out_shape=jax.ShapeDtypeStruct((m, heads * vd), BF16),
        scratch_shapes=[
            pltpu.VMEM((2, tq, V7X_LANES), F32),
            pltpu.VMEM((2, tq, V7X_LANES), F32),
            pltpu.VMEM((2, tq, vd), F32),
            pltpu.VMEM((2, tq, tq), F32),
            pltpu.VMEM((2, tq, tq), F32),
        ],
        compiler_params=_params("arbitrary", "arbitrary", "arbitrary"),
        name="diff_attn",
    )(qkv, qkv, qkv, vec(lq1), vec(lk1), vec(lq2), vec(lk2), subln_g.reshape(1, vd))


def _proj_res_kernel(a_ref, w_ref, r_ref, o_ref):
    o_ref[...] = r_ref[...] + _dot(a_ref[...], w_ref[...])


def _proj_res(a, w, res, *, tm):
    m, k = a.shape
    n = w.shape[1]
    return pl.pallas_call(
        _proj_res_kernel,
        grid=(m // tm,),
        in_specs=[
            pl.BlockSpec((tm, k), lambda i: (i, 0)),
            pl.BlockSpec((k, n), lambda i: (0, 0)),
            pl.BlockSpec((tm, n), lambda i: (i, 0)),
        ],
        out_specs=pl.BlockSpec((tm, n), lambda i: (i, 0)),
        out_shape=jax.ShapeDtypeStruct((m, n), F32),
        compiler_params=_params("arbitrary"),
        name="attn_out",
    )(a, w, res)


def _cast_kernel(w_ref, o_ref, *, rows, cols, swap):
    tr, tc = o_ref.shape
    x = w_ref[...]
    if rows % tr or cols % tc:
        r = pl.program_id(0) * tr + lax.broadcasted_iota(jnp.int32, x.shape, 0)
        c = pl.program_id(1) * tc + lax.broadcasted_iota(jnp.int32, x.shape, 1)
        x = jnp.where((r < rows) & (c < cols), x, 0.0)
    if swap is None:
        o_ref[...] = x.astype(BF16)
        return
    head_dim, lo, hi, n, limit = swap
    in_qk = pl.program_id(1) * tc < limit
    lane = lax.broadcasted_iota(jnp.int32, (tr, head_dim), 1)
    take_hi = (lane >= lo) & (lane < lo + n) & in_qk
    take_lo = (lane >= hi) & (lane < hi + n) & in_qk
    for h in range(tc // head_dim):
        seg = x[:, h * head_dim:(h + 1) * head_dim]
        seg = jnp.where(take_hi, pltpu.roll(seg, head_dim - (hi - lo), 1),
                        jnp.where(take_lo, pltpu.roll(seg, hi - lo, 1), seg))
        o_ref[:, h * head_dim:(h + 1) * head_dim] = seg.astype(BF16)


def _cast_weight(w_stack, layer, *, pad_rows_to=1, pad_cols_to=1, swap=None):
    _, rows, cols = w_stack.shape
    rp = -(-rows // pad_rows_to) * pad_rows_to
    cp = -(-cols // pad_cols_to) * pad_cols_to
    tc = _pick(cp, (2048, 1024, 512, 256, 128))
    tr = _pick(rp, tuple(t for t in (2048, 1024, 512, 256, 128, 64, 32, 16) if t * tc <= CAST_BLOCK_ELEMS))
    if swap is not None:
        assert swap[4] % tc == 0 and tc % swap[0] == 0
    kern = functools.partial(_cast_kernel, rows=rows, cols=cols, swap=swap)
    return pl.pallas_call(
        kern,
        grid=(rp // tr, cp // tc),
        in_specs=[pl.BlockSpec((None, tr, tc), lambda i, j: (layer, jnp.minimum(i, (rows - 1) // tr),
                                                              jnp.minimum(j, (cols - 1) // tc)))],
        out_specs=pl.BlockSpec((tr, tc), lambda i, j: (i, j)),
        out_shape=jax.ShapeDtypeStruct((rp, cp), BF16),
        compiler_params=_params("arbitrary", "arbitrary"),
        name="cast_weight",
    )(w_stack)


def _pad_cols(a, n):
    return a if a.shape[-1] == n else jnp.pad(a, [(0, 0)] * (a.ndim - 1) + [(0, n - a.shape[-1])])


def kernel(x, positions, norm_mix, norm_ffn, norm_final, conv_w_in, conv_b_in, conv_dw, conv_dw_b, conv_ln_g, conv_ln_b, conv_w_out, conv_b_out, attn_w_qkv, attn_lq1, attn_lk1, attn_lq2, attn_lk2, attn_subln_g, attn_w_o, ffn_w_gate, ffn_w_up, ffn_dw, ffn_dw_b, ffn_w_down):
    batch, seq_len, d = x.shape
    depth = norm_mix.shape[0]
    m = batch * seq_len
    d_ff = ffn_w_gate.shape[2]
    head_dim = attn_lq1.shape[1]
    heads = d // (2 * head_dim)
    qk_width = 2 * heads * head_dim

    tm = _pick(seq_len, (1024, 512, 256, 128, 64, 32))
    tm_proj = _pick(seq_len, (512, 256, 128, 64, 32))
    tm_conv = _pick(seq_len, (256, 128, 64))
    tn = _pick(d, (512, 256))
    tq = _pick(seq_len, (512, 256))
    tf = 2 * V7X_MXU_DIM
    f_pad = -(-d_ff // tf) * tf

    x2 = x.reshape(m, d)
    for i in range(depth):
        j = i // N_MIXERS
        if i % N_MIXERS == 0:
            glu = _conf_in(x2, norm_mix[i], _cast_weight(conv_w_in, j), conv_b_in[j], tm=tm, tn=tn)
            x2 = _conf_out(glu, x2, conv_dw[j], conv_dw_b[j], conv_ln_g[j], conv_ln_b[j],
                           _cast_weight(conv_w_out, j), conv_b_out[j], tm=tm_conv, seq_len=seq_len)
        else:
            lam_init = 0.8 - 0.6 * math.exp(-0.3 * i)
            (lo, hi, n), invf, half = _rope_layout(head_dim)
            w_qkv = _cast_weight(attn_w_qkv, j, swap=(head_dim, lo, hi, n, 2 * qk_width))
            qkv = _qkv(x2, norm_mix[i], positions, w_qkv, invf,
                       tm=tm, tn=tn, head_dim=head_dim, half=half, qk_width=qk_width)
            o = _attn(qkv, attn_lq1[j], attn_lk1[j], attn_lq2[j], attn_lk2[j], attn_subln_g[j],
                      batch=batch, seq_len=seq_len, heads=heads, head_dim=head_dim, tq=tq,
                      lam_init=lam_init)
            x2 = _proj_res(o, _cast_weight(attn_w_o, j), x2, tm=tm_proj)
        final_g = norm_final if i == depth - 1 else None
        x2 = _ffn(x2, norm_ffn[i],
                  _cast_weight(ffn_w_gate, i, pad_cols_to=tf),
                  _cast_weight(ffn_w_up, i, pad_cols_to=tf),
                  _pad_cols(ffn_dw[i], f_pad), _pad_cols(ffn_dw_b[i], f_pad),
                  _cast_weight(ffn_w_down, i, pad_rows_to=tf),
                  final_g, tm=tm, tf=tf, seq_len=seq_len)
    if depth == 0:
        raise ValueError("depth must be positive")
    return x2.reshape(batch, seq_len, d)
```

```python
import functools
import math

import jax
import jax.numpy as jnp
from jax import lax
from jax.experimental import pallas as pl
from jax.experimental.pallas import tpu as pltpu

RMS_EPS = 1e-6
LN_EPS = 1e-5
ROPE_THETA = 500000.0
N_MIXERS = 2

V7X_LANES = 128
SUBLANES = 8
V7X_MXU_DIM = 256
V7X_VMEM_LIMIT_BYTES = 60 * 1024 * 1024

F32 = jnp.float32
BF16 = jnp.bfloat16
MASK_VALUE = -1e30

CONV_HALO = 32
CONV_ROWS = 64
CONV_LANES = 256
LN_ROWS = 32
LN_UNROLL = 8
FFN_HALO = 8
CAST_BLOCK_ELEMS = 1024 * 1024


def _pick(n, candidates):
    for c in candidates:
        if n % c == 0:
            return c
    raise ValueError(f"no tile in {candidates} divides {n}")


def _params(*sem):
    return pltpu.CompilerParams(dimension_semantics=sem,
                                vmem_limit_bytes=V7X_VMEM_LIMIT_BYTES)


def _rms(x, g):
    ms = jnp.mean(x * x, axis=-1, keepdims=True)
    return x * lax.rsqrt(ms + RMS_EPS) * g


def _dot(a, b):
    return jnp.dot(a, b, preferred_element_type=F32)


def _conf_in_kernel(x_ref, ng_ref, wa_ref, wb_ref, ba_ref, bb_ref, o_ref, hn_ref):
    @pl.when(pl.program_id(1) == 0)
    def _():
        hn_ref[...] = _rms(x_ref[...], ng_ref[...]).astype(BF16)

    hn = hn_ref[...]
    for c in range(o_ref.shape[1] // V7X_MXU_DIM):
        cols = slice(c * V7X_MXU_DIM, (c + 1) * V7X_MXU_DIM)
        a = _dot(hn, wa_ref[:, cols]) + ba_ref[:, cols]
        gate = _dot(hn, wb_ref[:, cols]) + bb_ref[:, cols]
        o_ref[:, cols] = a * jax.nn.sigmoid(gate)


def _conf_in(x2, norm_g, w_in, b_in, *, tm, tn):
    m, d = x2.shape
    nj = d // tn
    return pl.pallas_call(
        _conf_in_kernel,
        grid=(m // tm, nj),
        in_specs=[
            pl.BlockSpec((tm, d), lambda i, j: (i, 0)),
            pl.BlockSpec((1, d), lambda i, j: (0, 0)),
            pl.BlockSpec((d, tn), lambda i, j: (0, j)),
            pl.BlockSpec((d, tn), lambda i, j: (0, j + nj)),
            pl.BlockSpec((1, tn), lambda i, j: (0, j)),
            pl.BlockSpec((1, tn), lambda i, j: (0, j + nj)),
        ],
        out_specs=pl.BlockSpec((tm, tn), lambda i, j: (i, j)),
        out_shape=jax.ShapeDtypeStruct((m, d), F32),
        scratch_shapes=[pltpu.VMEM((tm, d), BF16)],
        compiler_params=_params("arbitrary", "arbitrary"),
        name="conf_in",
    )(x2, norm_g.reshape(1, d), w_in, w_in, b_in.reshape(1, 2 * d), b_in.reshape(1, 2 * d))


def _conf_out_kernel(glu_ref, halo_ref, x_ref, dw_ref, dwb_ref, lng_ref, lnb_ref,
                     wo_ref, bo_ref, o_ref, cbuf_ref, xs_ref, wb_ref, h_ref, y_ref, *, tm, seq_len, width):
    d = glu_ref.shape[1]
    seq_start = (pl.program_id(0) * tm) % seq_len == 0

    @pl.when(pl.program_id(0) == 0)
    def _():
        for k in range(width):
            wb_ref[k] = jnp.broadcast_to(dw_ref[k:k + 1, :], (SUBLANES, d))

    @pl.when(seq_start)
    def _():
        cbuf_ref[0:CONV_HALO, :] = jnp.zeros((CONV_HALO, d), F32)

    @pl.when(jnp.logical_not(seq_start))
    def _():
        cbuf_ref[0:CONV_HALO, :] = halo_ref[...]

    cbuf_ref[CONV_HALO:, :] = glu_ref[...]

    base = CONV_HALO - (width - 1)
    def lane_block(c, carry):
        lanes = pl.ds(pl.multiple_of(c * CONV_LANES, CONV_LANES), CONV_LANES)
        for s in range(min(SUBLANES, width)):
            span = tm + SUBLANES * ((width - 1 - s) // SUBLANES)
            xs_ref[s, 0:span, :] = cbuf_ref[base + s:base + s + span, lanes]
        def conv_rows(rb, carry2):
            r0 = pl.multiple_of(rb * CONV_ROWS, CONV_ROWS)
            groups = CONV_ROWS // SUBLANES
            accs = [jnp.zeros((SUBLANES, CONV_LANES), F32)] * groups
            for k in range(width):
                a, s = divmod(k, SUBLANES)
                w8 = wb_ref[k, :, lanes]
                for g in range(groups):
                    accs[g] = accs[g] + w8 * xs_ref[s, pl.ds(r0 + SUBLANES * (a + g), SUBLANES), :]
            bias = dwb_ref[:, lanes]
            for g in range(groups):
                h_ref[pl.ds(r0 + SUBLANES * g, SUBLANES), lanes] = accs[g] + bias
            return carry2

        lax.fori_loop(0, tm // CONV_ROWS, conv_rows, 0)
        return carry

    lax.fori_loop(0, d // CONV_LANES, lane_block, 0)

    def row_block(rb, carry):
        r0 = pl.multiple_of(rb * LN_ROWS, LN_ROWS)
        h = h_ref[pl.ds(r0, LN_ROWS), :]
        mu = jnp.mean(h, axis=-1, keepdims=True)
        hc = h - mu
        var = jnp.mean(hc * hc, axis=-1, keepdims=True)
        y = hc * lax.rsqrt(var + LN_EPS) * lng_ref[...] + lnb_ref[...]
        y_ref[pl.ds(r0, LN_ROWS), :] = (y * jax.nn.sigmoid(y)).astype(BF16)
        return carry

    trips = tm // LN_ROWS
    lax.fori_loop(0, trips, row_block, 0, unroll=math.gcd(trips, LN_UNROLL))
    o_ref[...] = x_ref[...] + _dot(y_ref[...], wo_ref[...]) + bo_ref[...]


def _conf_out(glu, x2, dw, dwb, ln_g, ln_b, w_out, b_out, *, tm, seq_len):
    m, d = x2.shape
    width = dw.shape[0]
    assert width - 1 <= CONV_HALO and tm % CONV_HALO == 0 and tm % CONV_ROWS == 0 and tm % LN_ROWS == 0
    assert d % CONV_LANES == 0
    hb = tm // CONV_HALO
    row = lambda a: a.reshape(1, d)
    kern = functools.partial(_conf_out_kernel, tm=tm, seq_len=seq_len, width=width)
    return pl.pallas_call(
        kern,
        grid=(m // tm,),
        in_specs=[
            pl.BlockSpec((tm, d), lambda i: (i, 0)),
            pl.BlockSpec((CONV_HALO, d), lambda i: (jnp.maximum(i * hb - 1, 0), 0)),
            pl.BlockSpec((tm, d), lambda i: (i, 0)),
            pl.BlockSpec((width, d), lambda i: (0, 0)),
            pl.BlockSpec((1, d), lambda i: (0, 0)),
            pl.BlockSpec((1, d), lambda i: (0, 0)),
            pl.BlockSpec((1, d), lambda i: (0, 0)),
            pl.BlockSpec((d, d), lambda i: (0, 0)),
            pl.BlockSpec((1, d), lambda i: (0, 0)),
        ],
        out_specs=pl.BlockSpec((tm, d), lambda i: (i, 0)),
        out_shape=jax.ShapeDtypeStruct((m, d), F32),
        scratch_shapes=[pltpu.VMEM((CONV_HALO + tm, d), F32),
                        pltpu.VMEM((SUBLANES, CONV_HALO + tm, CONV_LANES), F32),
                        pltpu.VMEM((width, SUBLANES, d), F32),
                        pltpu.VMEM((tm, d), F32),
                        pltpu.VMEM((tm, d), BF16)],
        compiler_params=_params("arbitrary"),
        name="conf_out",
    )(glu, glu, x2, dw, row(dwb), row(ln_g), row(ln_b), w_out, row(b_out))


def _ffn_kernel(x_ref, ng_ref, wg_ref, wu_ref, dw_ref, dwb_ref, wd_ref, fg_ref, o_ref,
                hn_ref, gbuf_ref, carry_ref, *, tm, seq_len, final_norm):
    i = pl.program_id(0)
    j = pl.program_id(1)

    @pl.when(j == 0)
    def _():
        x = x_ref[...]
        hn_ref[...] = _rms(x, ng_ref[...]).astype(BF16)
        o_ref[...] = x

    tf = wg_ref.shape[1]
    seq_start = (i * tm) % seq_len == 0

    @pl.when(seq_start)
    def _():
        gbuf_ref[0:FFN_HALO, :] = jnp.zeros((FFN_HALO, tf), F32)

    @pl.when(jnp.logical_not(seq_start))
    def _():
        gbuf_ref[0:FFN_HALO, :] = carry_ref[j]

    hn = hn_ref[...]
    down = None
    for c in range(tf // V7X_MXU_DIM):
        cols = slice(c * V7X_MXU_DIM, (c + 1) * V7X_MXU_DIM)
        up = _dot(hn, wu_ref[:, cols])
        g_lin = _dot(hn, wg_ref[:, cols])
        gbuf_ref[FFN_HALO:, cols] = g_lin
        carry_ref[j, :, cols] = g_lin[tm - FFN_HALO:, :]
        g = (dw_ref[2:3, cols] * g_lin
             + dw_ref[1:2, cols] * gbuf_ref[FFN_HALO - 1:FFN_HALO - 1 + tm, cols]
             + dw_ref[0:1, cols] * gbuf_ref[FFN_HALO - 2:FFN_HALO - 2 + tm, cols]
             + dwb_ref[:, cols])
        act = (g * jax.nn.sigmoid(g) * up).astype(BF16)
        part = _dot(act, wd_ref[cols, :])
        down = part if down is None else down + part
    o_ref[...] += down

    if final_norm:
        @pl.when(j == pl.num_programs(1) - 1)
        def _():
            o_ref[...] = _rms(o_ref[...], fg_ref[...])


def _ffn(x2, norm_g, w_gate, w_up, dw, dwb, w_down, final_g, *, tm, tf, seq_len):
    m, d = x2.shape
    fp = w_gate.shape[1]
    assert dw.shape[0] == 3 and fp % tf == 0
    nj = fp // tf
    final_norm = final_g is not None
    fg = (final_g if final_norm else jnp.ones((d,), F32)).reshape(1, d)
    kern = functools.partial(_ffn_kernel, tm=tm, seq_len=seq_len, final_norm=final_norm)
    return pl.pallas_call(
        kern,
        grid=(m // tm, nj),
        in_specs=[
            pl.BlockSpec((tm, d), lambda i, j: (i, 0)),
            pl.BlockSpec((1, d), lambda i, j: (0, 0)),
            pl.BlockSpec((d, tf), lambda i, j: (0, j)),
            pl.BlockSpec((d, tf), lambda i, j: (0, j)),
            pl.BlockSpec((3, tf), lambda i, j: (0, j)),
            pl.BlockSpec((1, tf), lambda i, j: (0, j)),
            pl.BlockSpec((tf, d), lambda i, j: (j, 0)),
            pl.BlockSpec((1, d), lambda i, j: (0, 0)),
        ],
        out_specs=pl.BlockSpec((tm, d), lambda i, j: (i, 0)),
        out_shape=jax.ShapeDtypeStruct((m, d), F32),
        scratch_shapes=[
            pltpu.VMEM((tm, d), BF16),
            pltpu.VMEM((FFN_HALO + tm, tf), F32),
            pltpu.VMEM((nj, FFN_HALO, tf), F32),
        ],
        compiler_params=_params("arbitrary", "arbitrary"),
        name="ffn",
    )(x2, norm_g.reshape(1, d), w_gate, w_up, dw, dwb.reshape(1, fp), w_down, fg)


def _qkv_kernel(x_ref, ng_ref, pos_ref, invf_ref, w_ref, o_ref, hn_ref, cos_ref, sin_ref,
                *, tiles_per_kind, head_dim, half, q_scale):
    j = pl.program_id(1)
    tn = o_ref.shape[1]
    pair = head_dim // 2

    @pl.when(j == 0)
    def _():
        hn_ref[...] = _rms(x_ref[...], ng_ref[...]).astype(BF16)
        ang = pos_ref[...].astype(F32) * invf_ref[...]
        lane = lax.broadcasted_iota(jnp.int32, ang.shape, 1)
        cos = jnp.cos(ang)
        sin = jnp.where(lane < half, -jnp.sin(ang), jnp.sin(ang))
        cos_ref[0] = cos * q_scale
        sin_ref[0] = sin * q_scale
        cos_ref[1] = cos
        sin_ref[1] = sin
        cos_ref[2] = jnp.ones_like(cos)
        sin_ref[2] = jnp.zeros_like(sin)

    kind = jnp.minimum(j // tiles_per_kind, 2)
    cos = cos_ref[kind]
    sin = sin_ref[kind]
    hn = hn_ref[...]
    for c in range(tn // V7X_MXU_DIM):
        acc = _dot(hn, w_ref[:, c * V7X_MXU_DIM:(c + 1) * V7X_MXU_DIM])
        for h in range(V7X_MXU_DIM // head_dim):
            seg = acc[:, h * head_dim:(h + 1) * head_dim]
            col = c * V7X_MXU_DIM + h * head_dim
            o_ref[:, col:col + head_dim] = (seg * cos + pltpu.roll(seg, pair, 1) * sin).astype(BF16)


def _rope_layout(head_dim):
    rot = head_dim // 4
    half = rot // 2
    pair = head_dim // 2
    inv_freq = ROPE_THETA ** (-jnp.arange(half, dtype=F32) / half)
    invf = jnp.zeros((head_dim,), F32).at[:half].set(inv_freq).at[pair:pair + half].set(inv_freq)
    return (half, pair, half), invf.reshape(1, head_dim), half


def _qkv(x2, norm_g, positions, w_qkv, invf, *, tm, tn, head_dim, half, qk_width):
    m, d = x2.shape
    n = w_qkv.shape[1]
    assert head_dim == V7X_LANES and qk_width % tn == 0 and n % tn == 0 and tn % V7X_MXU_DIM == 0
    kern = functools.partial(_qkv_kernel, tiles_per_kind=qk_width // tn, head_dim=head_dim, half=half,
                             q_scale=head_dim ** -0.5 * math.log2(math.e))
    return pl.pallas_call(
        kern,
        grid=(m // tm, n // tn),
        in_specs=[
            pl.BlockSpec((tm, d), lambda i, j: (i, 0)),
            pl.BlockSpec((1, d), lambda i, j: (0, 0)),
            pl.BlockSpec((tm, 1), lambda i, j: (i, 0)),
            pl.BlockSpec((1, head_dim), lambda i, j: (0, 0)),
            pl.BlockSpec((d, tn), lambda i, j: (0, j)),
        ],
        out_specs=pl.BlockSpec((tm, tn), lambda i, j: (i, j)),
        out_shape=jax.ShapeDtypeStruct((m, n), BF16),
        scratch_shapes=[pltpu.VMEM((tm, d), BF16)] + [pltpu.VMEM((3, tm, head_dim), F32)] * 2,
        compiler_params=_params("arbitrary", "arbitrary"),
        name="qkv",
    )(x2, norm_g.reshape(1, d), positions.reshape(m, 1), invf, w_qkv)


def _attn_kernel(q_ref, k_ref, v_ref, lq1_ref, lk1_ref, lq2_ref, lk2_ref, sg_ref, o_ref,
                 m_ref, l_ref, acc_ref, sa_ref, sb_ref, ma_ref, mb_ref, *, tq, head_dim, lam_init):
    qi = pl.program_id(2)
    vd = 2 * head_dim
    m_ref[...] = jnp.full(m_ref.shape, MASK_VALUE, F32)
    l_ref[...] = jnp.zeros(l_ref.shape, F32)
    acc_ref[...] = jnp.zeros(acc_ref.shape, F32)

    def scores(kj, buf):
        s_ref, mt_ref = buf
        k0 = pl.multiple_of(kj * tq, tq)
        for idx in range(2):
            cols = slice(idx * head_dim, (idx + 1) * head_dim)
            s = lax.dot_general(q_ref[:, cols], k_ref[pl.ds(k0, tq), cols],
                                (((1,), (1,)), ((), ())), preferred_element_type=F32)
            s_ref[idx] = s
            mt_ref[idx] = jnp.broadcast_to(jnp.max(s, axis=1, keepdims=True), (tq, V7X_LANES))

    def fold(kj, buf, masked):
        s_ref, mt_ref = buf
        v = v_ref[pl.ds(pl.multiple_of(kj * tq, tq), tq), :]
        for idx in range(2):
            s = s_ref[idx]
            if masked:
                row = lax.broadcasted_iota(jnp.int32, s.shape, 0)
                col = lax.broadcasted_iota(jnp.int32, s.shape, 1)
                s = jnp.where(col <= row, s, MASK_VALUE)
                m_tile = jnp.max(s, axis=1, keepdims=True)
            else:
                m_tile = mt_ref[idx]
            m_prev = m_ref[idx]
            m_new = jnp.maximum(m_prev, m_tile)
            alpha = jnp.exp2(m_prev - m_new)
            ps = [jnp.exp2(s[:, c * V7X_LANES:(c + 1) * V7X_LANES] - m_new)
                  for c in range(tq // V7X_LANES)]
            l_ref[idx] = alpha * l_ref[idx] + functools.reduce(jnp.add, ps)
            pv = _dot(jnp.concatenate(ps, axis=1).astype(BF16), v)
            alpha_v = jnp.concatenate([alpha] * (vd // V7X_LANES), axis=1)
            acc_ref[idx] = alpha_v * acc_ref[idx] + pv
            m_ref[idx] = m_new

    buf_a = (sa_ref, ma_ref)
    buf_b = (sb_ref, mb_ref)
    scores(0, buf_a)

    def pair(p, carry):
        scores(2 * p + 1, buf_b)
        fold(2 * p, buf_a, False)
        scores(2 * p + 2, buf_a)
        fold(2 * p + 1, buf_b, False)
        return carry

    lax.fori_loop(0, lax.shift_right_logical(qi, jnp.int32(1)), pair, 0)

    @pl.when((qi & 1) == 0)
    def _():
        fold(qi, buf_a, True)

    @pl.when((qi & 1) == 1)
    def _():
        scores(qi, buf_b)
        fold(qi - 1, buf_a, False)
        fold(qi, buf_b, True)

    lam = (jnp.exp(jnp.sum(lq1_ref[...] * lk1_ref[...], keepdims=True))
           - jnp.exp(jnp.sum(lq2_ref[...] * lk2_ref[...], keepdims=True)) + lam_init)
    l0 = jnp.sum(l_ref[0], axis=1, keepdims=True)
    l1 = jnp.sum(l_ref[1], axis=1, keepdims=True)
    o = acc_ref[0] / l0 - lam * (acc_ref[1] / l1)
    o_ref[...] = (_rms(o, sg_ref[...]) * (1.0 - lam_init)).astype(BF16)


def _attn(qkv, lq1, lk1, lq2, lk2, subln_g, *, batch, seq_len, heads, head_dim, tq, lam_init):
    m = qkv.shape[0]
    vd = 2 * head_dim
    nq = seq_len // tq
    k_col0 = heads
    v_col0 = 2 * heads
    vec = lambda a: a.reshape(1, head_dim)
    kern = functools.partial(_attn_kernel, tq=tq, head_dim=head_dim, lam_init=lam_init)
    small = pl.BlockSpec((1, head_dim), lambda b, h, q: (0, 0))
    return pl.pallas_call(
        kern,
        grid=(batch, heads, nq),
        in_specs=[
            pl.BlockSpec((tq, vd), lambda b, h, q: (b * nq + q, h)),
            pl.BlockSpec((seq_len, vd), lambda b, h, q: (b, k_col0 + h)),
            pl.BlockSpec((seq_len, vd), lambda b, h, q: (b, v_col0 + h)),
            small, small, small, small,
            pl.BlockSpec((1, vd), lambda b, h, q: (0, 0)),
        ],
        out_specs=pl.BlockSpec((tq, vd), lambda b, h, q: (b * nq + q, h)),
        out_shape=jax.ShapeDtypeStruct((m, heads * vd), BF16),
        scratch_shapes=[
            pltpu.VMEM((2, tq, V7X_LANES), F32),
            pltpu.VMEM((2, tq, V7X_LANES), F32),
            pltpu.VMEM((2, tq, vd), F32),
            pltpu.VMEM((2, tq, tq), F32),
            pltpu.VMEM((2, tq, tq), F32),
            pltpu.VMEM((2, tq, V7X_LANES), F32),
            pltpu.VMEM((2, tq, V7X_LANES), F32),
        ],
        compiler_params=_params("arbitrary", "arbitrary", "arbitrary"),
        name="diff_attn",
    )(qkv, qkv, qkv, vec(lq1), vec(lk1), vec(lq2), vec(lk2), subln_g.reshape(1, vd))


def _proj_res_kernel(a_ref, w_ref, r_ref, o_ref):
    o_ref[...] = r_ref[...] + _dot(a_ref[...], w_ref[...])


def _proj_res(a, w, res, *, tm):
    m, k = a.shape
    n = w.shape[1]
    return pl.pallas_call(
        _proj_res_kernel,
        grid=(m // tm,),
        in_specs=[
            pl.BlockSpec((tm, k), lambda i: (i, 0)),
            pl.BlockSpec((k, n), lambda i: (0, 0)),
            pl.BlockSpec((tm, n), lambda i: (i, 0)),
        ],
        out_specs=pl.BlockSpec((tm, n), lambda i: (i, 0)),
        out_shape=jax.ShapeDtypeStruct((m, n), F32),
        compiler_params=_params("arbitrary"),
        name="attn_out",
    )(a, w, res)


def _cast_kernel(w_ref, o_ref, *, rows, cols, swap):
    tr, tc = o_ref.shape
    x = w_ref[...]
    if rows % tr or cols % tc:
        r = pl.program_id(0) * tr + lax.broadcasted_iota(jnp.int32, x.shape, 0)
        c = pl.program_id(1) * tc + lax.broadcasted_iota(jnp.int32, x.shape, 1)
        x = jnp.where((r < rows) & (c < cols), x, 0.0)
    if swap is None:
        o_ref[...] = x.astype(BF16)
        return
    head_dim, lo, hi, n, limit = swap
    in_qk = pl.program_id(1) * tc < limit
    lane = lax.broadcasted_iota(jnp.int32, (tr, head_dim), 1)
    take_hi = (lane >= lo) & (lane < lo + n) & in_qk
    take_lo = (lane >= hi) & (lane < hi + n) & in_qk
    for h in range(tc // head_dim):
        seg = x[:, h * head_dim:(h + 1) * head_dim]
        seg = jnp.where(take_hi, pltpu.roll(seg, head_dim - (hi - lo), 1),
                        jnp.where(take_lo, pltpu.roll(seg, hi - lo, 1), seg))
        o_ref[:, h * head_dim:(h + 1) * head_dim] = seg.astype(BF16)


def _cast_weight(w_stack, layer, *, pad_rows_to=1, pad_cols_to=1, swap=None):
    _, rows, cols = w_stack.shape
    rp = -(-rows // pad_rows_to) * pad_rows_to
    cp = -(-cols // pad_cols_to) * pad_cols_to
    tc = _pick(cp, (2048, 1024, 512, 256, 128))
    tr = _pick(rp, tuple(t for t in (2048, 1024, 512, 256, 128, 64, 32, 16) if t * tc <= CAST_BLOCK_ELEMS))
    if swap is not None:
        assert swap[4] % tc == 0 and tc % swap[0] == 0
    kern = functools.partial(_cast_kernel, rows=rows, cols=cols, swap=swap)
    return pl.pallas_call(
        kern,
        grid=(rp // tr, cp // tc),
        in_specs=[pl.BlockSpec((None, tr, tc), lambda i, j: (layer, jnp.minimum(i, (rows - 1) // tr),
                                                              jnp.minimum(j, (cols - 1) // tc)))],
        out_specs=pl.BlockSpec((tr, tc), lambda i, j: (i, j)),
        out_shape=jax.ShapeDtypeStruct((rp, cp), BF16),
        compiler_params=_params("arbitrary", "arbitrary"),
        name="cast_weight",
    )(w_stack)


def _pad_cols(a, n):
    return a if a.shape[-1] == n else jnp.pad(a, [(0, 0)] * (a.ndim - 1) + [(0, n - a.shape[-1])])


def kernel(x, positions, norm_mix, norm_ffn, norm_final, conv_w_in, conv_b_in, conv_dw, conv_dw_b, conv_ln_g, conv_ln_b, conv_w_out, conv_b_out, attn_w_qkv, attn_lq1, attn_lk1, attn_lq2, attn_lk2, attn_subln_g, attn_w_o, ffn_w_gate, ffn_w_up, ffn_dw, ffn_dw_b, ffn_w_down):
    batch, seq_len, d = x.shape
    depth = norm_mix.shape[0]
    m = batch * seq_len
    d_ff = ffn_w_gate.shape[2]
    head_dim = attn_lq1.shape[1]
    heads = d // (2 * head_dim)
    qk_width = 2 * heads * head_dim

    tm = _pick(seq_len, (1024, 512, 256, 128, 64, 32))
    tm_proj = _pick(seq_len, (512, 256, 128, 64, 32))
    tm_conv = _pick(seq_len, (256, 128, 64))
    tn = _pick(d, (512, 256))
    tn_qkv = _pick(qk_width, (1024, 512, 256))
    tq = _pick(seq_len, (512, 256))
    tf = 2 * V7X_MXU_DIM
    f_pad = -(-d_ff // tf) * tf

    x2 = x.reshape(m, d)
    for i in range(depth):
        j = i // N_MIXERS
        if i % N_MIXERS == 0:
            glu = _conf_in(x2, norm_mix[i], _cast_weight(conv_w_in, j), conv_b_in[j], tm=tm, tn=tn)
            x2 = _conf_out(glu, x2, conv_dw[j], conv_dw_b[j], conv_ln_g[j], conv_ln_b[j],
                           _cast_weight(conv_w_out, j), conv_b_out[j], tm=tm_conv, seq_len=seq_len)
        else:
            lam_init = 0.8 - 0.6 * math.exp(-0.3 * i)
            (lo, hi, n), invf, half = _rope_layout(head_dim)
            w_qkv = _cast_weight(attn_w_qkv, j, swap=(head_dim, lo, hi, n, 2 * qk_width))
            qkv = _qkv(x2, norm_mix[i], positions, w_qkv, invf,
                       tm=tm, tn=tn_qkv, head_dim=head_dim, half=half, qk_width=qk_width)
            o = _attn(qkv, attn_lq1[j], attn_lk1[j], attn_lq2[j], attn_lk2[j], attn_subln_g[j],
                      batch=batch, seq_len=seq_len, heads=heads, head_dim=head_dim, tq=tq,
                      lam_init=lam_init)
            x2 = _proj_res(o, _cast_weight(attn_w_o, j), x2, tm=tm_proj)
        final_g = norm_final if i == depth - 1 else None
        x2 = _ffn(x2, norm_ffn[i],
                  _cast_weight(ffn_w_gate, i, pad_cols_to=tf),
                  _cast_weight(ffn_w_up, i, pad_cols_to=tf),
                  _pad_cols(ffn_dw[i], f_pad), _pad_cols(ffn_dw_b[i], f_pad),
                  _cast_weight(ffn_w_down, i, pad_rows_to=tf),
                  final_g, tm=tm, tf=tf, seq_len=seq_len)
    if depth == 0:
        raise ValueError("depth must be positive")
    return x2.reshape(batch, seq_len, d)
```

```python
import functools
import math

import jax
import jax.numpy as jnp
from jax import lax
from jax.experimental import pallas as pl
from jax.experimental.pallas import tpu as pltpu

RMS_EPS = 1e-6
LN_EPS = 1e-5
ROPE_THETA = 500000.0
N_MIXERS = 2

V7X_LANES = 128
SUBLANES = 8
V7X_MXU_DIM = 256
V7X_VMEM_LIMIT_BYTES = 60 * 1024 * 1024

F32 = jnp.float32
BF16 = jnp.bfloat16
MASK_VALUE = -1e30

CONV_HALO = 32
CONV_ROWS = 256
CONV_LANES = 128
LN_ROWS = 32
LN_UNROLL = 8
FFN_HALO = 8
CAST_BLOCK_ELEMS = 1024 * 1024


def _pick(n, candidates):
    for c in candidates:
        if n % c == 0:
            return c
    raise ValueError(f"no tile in {candidates} divides {n}")


def _params(*sem):
    return pltpu.CompilerParams(dimension_semantics=sem,
                                vmem_limit_bytes=V7X_VMEM_LIMIT_BYTES)


def _rms(x, g):
    ms = jnp.mean(x * x, axis=-1, keepdims=True)
    return x * lax.rsqrt(ms + RMS_EPS) * g


def _dot(a, b):
    return jnp.dot(a, b, preferred_element_type=F32)


def _conf_in_kernel(x_ref, ng_ref, wa_ref, wb_ref, ba_ref, bb_ref, o_ref, hn_ref):
    @pl.when(pl.program_id(1) == 0)
    def _():
        hn_ref[...] = _rms(x_ref[...], ng_ref[...]).astype(BF16)

    hn = hn_ref[...]
    for c in range(o_ref.shape[1] // V7X_MXU_DIM):
        cols = slice(c * V7X_MXU_DIM, (c + 1) * V7X_MXU_DIM)
        a = _dot(hn, wa_ref[:, cols]) + ba_ref[:, cols]
        gate = _dot(hn, wb_ref[:, cols]) + bb_ref[:, cols]
        o_ref[:, cols] = a * jax.nn.sigmoid(gate)


def _conf_in(x2, norm_g, w_in, b_in, *, tm, tn):
    m, d = x2.shape
    nj = d // tn
    return pl.pallas_call(
        _conf_in_kernel,
        grid=(m // tm, nj),
        in_specs=[
            pl.BlockSpec((tm, d), lambda i, j: (i, 0)),
            pl.BlockSpec((1, d), lambda i, j: (0, 0)),
            pl.BlockSpec((d, tn), lambda i, j: (0, j)),
            pl.BlockSpec((d, tn), lambda i, j: (0, j + nj)),
            pl.BlockSpec((1, tn), lambda i, j: (0, j)),
            pl.BlockSpec((1, tn), lambda i, j: (0, j + nj)),
        ],
        out_specs=pl.BlockSpec((tm, tn), lambda i, j: (i, j)),
        out_shape=jax.ShapeDtypeStruct((m, d), F32),
        scratch_shapes=[pltpu.VMEM((tm, d), BF16)],
        compiler_params=_params("arbitrary", "arbitrary"),
        name="conf_in",
    )(x2, norm_g.reshape(1, d), w_in, w_in, b_in.reshape(1, 2 * d), b_in.reshape(1, 2 * d))


def _conf_out_kernel(glu_ref, halo_ref, x_ref, dw_ref, dwb_ref, lng_ref, lnb_ref,
                     wo_ref, bo_ref, o_ref, cbuf_ref, xs_ref, wb_ref, h_ref, y_ref, *, tm, seq_len, width):
    d = glu_ref.shape[1]
    seq_start = (pl.program_id(0) * tm) % seq_len == 0

    @pl.when(pl.program_id(0) == 0)
    def _():
        for k in range(width):
            wb_ref[k] = jnp.broadcast_to(dw_ref[k:k + 1, :], (SUBLANES, d))

    @pl.when(seq_start)
    def _():
        cbuf_ref[0:CONV_HALO, :] = jnp.zeros((CONV_HALO, d), F32)

    @pl.when(jnp.logical_not(seq_start))
    def _():
        cbuf_ref[0:CONV_HALO, :] = halo_ref[...]

    cbuf_ref[CONV_HALO:, :] = glu_ref[...]

    base = CONV_HALO - (width - 1)
    conv_rows_blk = math.gcd(tm, CONV_ROWS)

    def lane_block(c, carry):
        lanes = pl.ds(pl.multiple_of(c * CONV_LANES, CONV_LANES), CONV_LANES)
        for s in range(min(SUBLANES, width)):
            span = tm + SUBLANES * ((width - 1 - s) // SUBLANES)
            xs_ref[s, 0:span, :] = cbuf_ref[base + s:base + s + span, lanes]
        def conv_rows(rb, carry2):
            r0 = pl.multiple_of(rb * conv_rows_blk, conv_rows_blk)
            groups = conv_rows_blk // SUBLANES
            accs = [jnp.zeros((SUBLANES, CONV_LANES), F32)] * groups
            for k in range(width):
                a, s = divmod(k, SUBLANES)
                w8 = wb_ref[k, :, lanes]
                for g in range(groups):
                    accs[g] = accs[g] + w8 * xs_ref[s, pl.ds(r0 + SUBLANES * (a + g), SUBLANES), :]
            bias = dwb_ref[:, lanes]
            for g in range(groups):
                h_ref[pl.ds(r0 + SUBLANES * g, SUBLANES), lanes] = accs[g] + bias
            return carry2

        lax.fori_loop(0, tm // conv_rows_blk, conv_rows, 0)
        return carry

    lax.fori_loop(0, d // CONV_LANES, lane_block, 0)

    def row_block(rb, carry):
        r0 = pl.multiple_of(rb * LN_ROWS, LN_ROWS)
        h = h_ref[pl.ds(r0, LN_ROWS), :]
        mu = jnp.mean(h, axis=-1, keepdims=True)
        hc = h - mu
        var = jnp.mean(hc * hc, axis=-1, keepdims=True)
        y = hc * lax.rsqrt(var + LN_EPS) * lng_ref[...] + lnb_ref[...]
        y_ref[pl.ds(r0, LN_ROWS), :] = (y * jax.nn.sigmoid(y)).astype(BF16)
        return carry

    trips = tm // LN_ROWS
    lax.fori_loop(0, trips, row_block, 0, unroll=math.gcd(trips, LN_UNROLL))
    o_ref[...] = x_ref[...] + _dot(y_ref[...], wo_ref[...]) + bo_ref[...]


def _conf_out(glu, x2, dw, dwb, ln_g, ln_b, w_out, b_out, *, tm, seq_len):
    m, d = x2.shape
    width = dw.shape[0]
    assert width - 1 <= CONV_HALO and tm % CONV_HALO == 0 and tm % LN_ROWS == 0
    assert d % CONV_LANES == 0
    hb = tm // CONV_HALO
    row = lambda a: a.reshape(1, d)
    kern = functools.partial(_conf_out_kernel, tm=tm, seq_len=seq_len, width=width)
    return pl.pallas_call(
        kern,
        grid=(m // tm,),
        in_specs=[
            pl.BlockSpec((tm, d), lambda i: (i, 0)),
            pl.BlockSpec((CONV_HALO, d), lambda i: (jnp.maximum(i * hb - 1, 0), 0)),
            pl.BlockSpec((tm, d), lambda i: (i, 0)),
            pl.BlockSpec((width, d), lambda i: (0, 0)),
            pl.BlockSpec((1, d), lambda i: (0, 0)),
            pl.BlockSpec((1, d), lambda i: (0, 0)),
            pl.BlockSpec((1, d), lambda i: (0, 0)),
            pl.BlockSpec((d, d), lambda i: (0, 0)),
            pl.BlockSpec((1, d), lambda i: (0, 0)),
        ],
        out_specs=pl.BlockSpec((tm, d), lambda i: (i, 0)),
        out_shape=jax.ShapeDtypeStruct((m, d), F32),
        scratch_shapes=[pltpu.VMEM((CONV_HALO + tm, d), F32),
                        pltpu.VMEM((SUBLANES, CONV_HALO + tm, CONV_LANES), F32),
                        pltpu.VMEM((width, SUBLANES, d), F32),
                        pltpu.VMEM((tm, d), F32),
                        pltpu.VMEM((tm, d), BF16)],
        compiler_params=_params("arbitrary"),
        name="conf_out",
    )(glu, glu, x2, dw, row(dwb), row(ln_g), row(ln_b), w_out, row(b_out))


def _ffn_kernel(x_ref, ng_ref, wg_ref, wu_ref, dw_ref, dwb_ref, wd_ref, fg_ref, o_ref,
                hn_ref, gbuf_ref, carry_ref, *, tm, seq_len, final_norm):
    i = pl.program_id(0)
    j = pl.program_id(1)

    @pl.when(j == 0)
    def _():
        x = x_ref[...]
        hn_ref[...] = _rms(x, ng_ref[...]).astype(BF16)
        o_ref[...] = x

    tf = wg_ref.shape[1]
    seq_start = (i * tm) % seq_len == 0

    @pl.when(seq_start)
    def _():
        gbuf_ref[0:FFN_HALO, :] = jnp.zeros((FFN_HALO, tf), F32)

    @pl.when(jnp.logical_not(seq_start))
    def _():
        gbuf_ref[0:FFN_HALO, :] = carry_ref[j]

    hn = hn_ref[...]
    down = None
    for c in range(tf // V7X_MXU_DIM):
        cols = slice(c * V7X_MXU_DIM, (c + 1) * V7X_MXU_DIM)
        up = _dot(hn, wu_ref[:, cols])
        g_lin = _dot(hn, wg_ref[:, cols])
        gbuf_ref[FFN_HALO:, cols] = g_lin
        carry_ref[j, :, cols] = g_lin[tm - FFN_HALO:, :]
        g = (dw_ref[2:3, cols] * g_lin
             + dw_ref[1:2, cols] * gbuf_ref[FFN_HALO - 1:FFN_HALO - 1 + tm, cols]
             + dw_ref[0:1, cols] * gbuf_ref[FFN_HALO - 2:FFN_HALO - 2 + tm, cols]
             + dwb_ref[:, cols])
        act = (g * jax.nn.sigmoid(g) * up).astype(BF16)
        part = _dot(act, wd_ref[cols, :])
        down = part if down is None else down + part
    o_ref[...] += down

    if final_norm:
        @pl.when(j == pl.num_programs(1) - 1)
        def _():
            o_ref[...] = _rms(o_ref[...], fg_ref[...])


def _ffn(x2, norm_g, w_gate, w_up, dw, dwb, w_down, final_g, *, tm, tf, seq_len):
    m, d = x2.shape
    fp = w_gate.shape[1]
    assert dw.shape[0] == 3 and fp % tf == 0
    nj = fp // tf
    final_norm = final_g is not None
    fg = (final_g if final_norm else jnp.ones((d,), F32)).reshape(1, d)
    kern = functools.partial(_ffn_kernel, tm=tm, seq_len=seq_len, final_norm=final_norm)
    return pl.pallas_call(
        kern,
        grid=(m // tm, nj),
        in_specs=[
            pl.BlockSpec((tm, d), lambda i, j: (i, 0)),
            pl.BlockSpec((1, d), lambda i, j: (0, 0)),
            pl.BlockSpec((d, tf), lambda i, j: (0, j)),
            pl.BlockSpec((d, tf), lambda i, j: (0, j)),
            pl.BlockSpec((3, tf), lambda i, j: (0, j)),
            pl.BlockSpec((1, tf), lambda i, j: (0, j)),
            pl.BlockSpec((tf, d), lambda i, j: (j, 0)),
            pl.BlockSpec((1, d), lambda i, j: (0, 0)),
        ],
        out_specs=pl.BlockSpec((tm, d), lambda i, j: (i, 0)),
        out_shape=jax.ShapeDtypeStruct((m, d), F32),
        scratch_shapes=[
            pltpu.VMEM((tm, d), BF16),
            pltpu.VMEM((FFN_HALO + tm, tf), F32),
            pltpu.VMEM((nj, FFN_HALO, tf), F32),
        ],
        compiler_params=_params("arbitrary", "arbitrary"),
        name="ffn",
    )(x2, norm_g.reshape(1, d), w_gate, w_up, dw, dwb.reshape(1, fp), w_down, fg)


def _qkv_kernel(x_ref, ng_ref, pos_ref, invf_ref, w_ref, o_ref, hn_ref, cos_ref, sin_ref,
                *, tiles_per_kind, head_dim, half, q_scale):
    j = pl.program_id(1)
    tn = o_ref.shape[1]
    pair = head_dim // 2

    @pl.when(j == 0)
    def _():
        hn_ref[...] = _rms(x_ref[...], ng_ref[...]).astype(BF16)
        ang = pos_ref[...].astype(F32) * invf_ref[...]
        lane = lax.broadcasted_iota(jnp.int32, ang.shape, 1)
        cos = jnp.cos(ang)
        sin = jnp.where(lane < half, -jnp.sin(ang), jnp.sin(ang))
        cos_ref[0] = cos * q_scale
        sin_ref[0] = sin * q_scale
        cos_ref[1] = cos
        sin_ref[1] = sin
        cos_ref[2] = jnp.ones_like(cos)
        sin_ref[2] = jnp.zeros_like(sin)

    kind = jnp.minimum(j // tiles_per_kind, 2)
    cos = cos_ref[kind]
    sin = sin_ref[kind]
    hn = hn_ref[...]
    for c in range(tn // V7X_MXU_DIM):
        acc = _dot(hn, w_ref[:, c * V7X_MXU_DIM:(c + 1) * V7X_MXU_DIM])
        for h in range(V7X_MXU_DIM // head_dim):
            seg = acc[:, h * head_dim:(h + 1) * head_dim]
            col = c * V7X_MXU_DIM + h * head_dim
            o_ref[:, col:col + head_dim] = (seg * cos + pltpu.roll(seg, pair, 1) * sin).astype(BF16)


def _rope_layout(head_dim):
    rot = head_dim // 4
    half = rot // 2
    pair = head_dim // 2
    inv_freq = ROPE_THETA ** (-jnp.arange(half, dtype=F32) / half)
    invf = jnp.zeros((head_dim,), F32).at[:half].set(inv_freq).at[pair:pair + half].set(inv_freq)
    return (half, pair, half), invf.reshape(1, head_dim), half


def _qkv(x2, norm_g, positions, w_qkv, invf, *, tm, tn, head_dim, half, qk_width):
    m, d = x2.shape
    n = w_qkv.shape[1]
    assert head_dim == V7X_LANES and qk_width % tn == 0 and n % tn == 0 and tn % V7X_MXU_DIM == 0
    kern = functools.partial(_qkv_kernel, tiles_per_kind=qk_width // tn, head_dim=head_dim, half=half,
                             q_scale=head_dim ** -0.5 * math.log2(math.e))
    return pl.pallas_call(
        kern,
        grid=(m // tm, n // tn),
        in_specs=[
            pl.BlockSpec((tm, d), lambda i, j: (i, 0)),
            pl.BlockSpec((1, d), lambda i, j: (0, 0)),
            pl.BlockSpec((tm, 1), lambda i, j: (i, 0)),
            pl.BlockSpec((1, head_dim), lambda i, j: (0, 0)),
            pl.BlockSpec((d, tn), lambda i, j: (0, j)),
        ],
        out_specs=pl.BlockSpec((tm, tn), lambda i, j: (i, j)),
        out_shape=jax.ShapeDtypeStruct((m, n), BF16),
        scratch_shapes=[pltpu.VMEM((tm, d), BF16)] + [pltpu.VMEM((3, tm, head_dim), F32)] * 2,
        compiler_params=_params("arbitrary", "arbitrary"),
        name="qkv",
    )(x2, norm_g.reshape(1, d), positions.reshape(m, 1), invf, w_qkv)


def _attn_kernel(q_ref, k_ref, v_ref, lq1_ref, lk1_ref, lq2_ref, lk2_ref, sg_ref, o_ref,
                 m_ref, l_ref, acc_ref, sa_ref, sb_ref, ma_ref, mb_ref, *, tq, head_dim, lam_init):
    qi = pl.program_id(2)
    vd = 2 * head_dim
    m_ref[...] = jnp.full(m_ref.shape, MASK_VALUE, F32)
    l_ref[...] = jnp.zeros(l_ref.shape, F32)
    acc_ref[...] = jnp.zeros(acc_ref.shape, F32)

    def scores(kj, buf):
        s_ref, mt_ref = buf
        k0 = pl.multiple_of(kj * tq, tq)
        for idx in range(2):
            cols = slice(idx * head_dim, (idx + 1) * head_dim)
            s = lax.dot_general(q_ref[:, cols], k_ref[pl.ds(k0, tq), cols],
                                (((1,), (1,)), ((), ())), preferred_element_type=F32)
            s_ref[idx] = s
            mt_ref[idx] = jnp.broadcast_to(jnp.max(s, axis=1, keepdims=True), (tq, V7X_LANES))

    def fold(kj, buf, masked=False):
        s_ref, mt_ref = buf
        v = v_ref[pl.ds(pl.multiple_of(kj * tq, tq), tq), :]
        for idx in range(2):
            s = s_ref[idx]
            if masked:
                row = lax.broadcasted_iota(jnp.int32, s.shape, 0)
                col = lax.broadcasted_iota(jnp.int32, s.shape, 1)
                s = jnp.where(col <= row, s, MASK_VALUE)
                m_tile = jnp.max(s, axis=1, keepdims=True)
            else:
                m_tile = mt_ref[idx]
            m_prev = m_ref[idx]
            m_new = jnp.maximum(m_prev, m_tile)
            alpha = jnp.exp2(m_prev - m_new)
            ps = [jnp.exp2(s[:, c * V7X_LANES:(c + 1) * V7X_LANES] - m_new)
                  for c in range(tq // V7X_LANES)]
            l_ref[idx] = alpha * l_ref[idx] + functools.reduce(jnp.add, ps)
            pv = _dot(jnp.concatenate(ps, axis=1).astype(BF16), v)
            alpha_v = jnp.concatenate([alpha] * (vd // V7X_LANES), axis=1)
            acc_ref[idx] = alpha_v * acc_ref[idx] + pv
            m_ref[idx] = m_new

    buf_a = (sa_ref, ma_ref)
    buf_b = (sb_ref, mb_ref)
    scores(0, buf_a)

    def pair(p, carry):
        scores(2 * p + 1, buf_b)
        fold(2 * p, buf_a)
        scores(2 * p + 2, buf_a)
        fold(2 * p + 1, buf_b)
        return carry

    lax.fori_loop(0, lax.shift_right_logical(qi, jnp.int32(1)), pair, 0)

    @pl.when((qi & 1) == 0)
    def _():
        fold(qi, buf_a, masked=True)

    @pl.when((qi & 1) == 1)
    def _():
        scores(qi, buf_b)
        fold(qi - 1, buf_a)
        fold(qi, buf_b, masked=True)

    lam = (jnp.exp(jnp.sum(lq1_ref[...] * lk1_ref[...], keepdims=True))
           - jnp.exp(jnp.sum(lq2_ref[...] * lk2_ref[...], keepdims=True)) + lam_init)
    l0 = jnp.sum(l_ref[0], axis=1, keepdims=True)
    l1 = jnp.sum(l_ref[1], axis=1, keepdims=True)
    o = acc_ref[0] / l0 - lam * (acc_ref[1] / l1)
    o_ref[...] = (_rms(o, sg_ref[...]) * (1.0 - lam_init)).astype(BF16)


def _attn(qkv, lq1, lk1, lq2, lk2, subln_g, *, batch, seq_len, heads, head_dim, tq, lam_init):
    m = qkv.shape[0]
    vd = 2 * head_dim
    nq = seq_len // tq
    k_col0 = heads
    v_col0 = 2 * heads
    vec = lambda a: a.reshape(1, head_dim)
    kern = functools.partial(_attn_kernel, tq=tq, head_dim=head_dim, lam_init=lam_init)
    small = pl.BlockSpec((1, head_dim), lambda b, h, q: (0, 0))
    return pl.pallas_call(
        kern,
        grid=(batch, heads, nq),
        in_specs=[
            pl.BlockSpec((tq, vd), lambda b, h, q: (b * nq + q, h)),
            pl.BlockSpec((seq_len, vd), lambda b, h, q: (b, k_col0 + h)),
            pl.BlockSpec((seq_len, vd), lambda b, h, q: (b, v_col0 + h)),
            small, small, small, small,
            pl.BlockSpec((1, vd), lambda b, h, q: (0, 0)),
        ],
        out_specs=pl.BlockSpec((tq, vd), lambda b, h, q: (b * nq + q, h)),
        out_shape=jax.ShapeDtypeStruct((m, heads * vd), BF16),
        scratch_shapes=[
            pltpu.VMEM((2, tq, V7X_LANES), F32),
            pltpu.VMEM((2, tq, V7X_LANES), F32),
            pltpu.VMEM((2, tq, vd), F32),
            pltpu.VMEM((2, tq, tq), F32),
            pltpu.VMEM((2, tq, tq), F32),
            pltpu.VMEM((2, tq, V7X_LANES), F32),
            pltpu.VMEM((2, tq, V7X_LANES), F32),
        ],
        compiler_params=_params("arbitrary", "arbitrary", "arbitrary"),
        name="diff_attn",
    )(qkv, qkv, qkv, vec(lq1), vec(lk1), vec(lq2), vec(lk2), subln_g.reshape(1, vd))


def _proj_res_kernel(a_ref, w_ref, r_ref, o_ref):
    o_ref[...] = r_ref[...] + _dot(a_ref[...], w_ref[...])


def _proj_res(a, w, res, *, tm):
    m, k = a.shape
    n = w.shape[1]
    return pl.pallas_call(
        _proj_res_kernel,
        grid=(m // tm,),
        in_specs=[
            pl.BlockSpec((tm, k), lambda i: (i, 0)),
            pl.BlockSpec((k, n), lambda i: (0, 0)),
            pl.BlockSpec((tm, n), lambda i: (i, 0)),
        ],
        out_specs=pl.BlockSpec((tm, n), lambda i: (i, 0)),
        out_shape=jax.ShapeDtypeStruct((m, n), F32),
        compiler_params=_params("arbitrary"),
        name="attn_out",
    )(a, w, res)


def _cast_kernel(w_ref, o_ref, *, rows, cols, swap):
    tr, tc = o_ref.shape
    x = w_ref[...]
    if rows % tr or cols % tc:
        r = pl.program_id(0) * tr + lax.broadcasted_iota(jnp.int32, x.shape, 0)
        c = pl.program_id(1) * tc + lax.broadcasted_iota(jnp.int32, x.shape, 1)
        x = jnp.where((r < rows) & (c < cols), x, 0.0)
    if swap is None:
        o_ref[...] = x.astype(BF16)
        return
    head_dim, lo, hi, n, limit = swap
    in_qk = pl.program_id(1) * tc < limit
    lane = lax.broadcasted_iota(jnp.int32, (tr, head_dim), 1)
    take_hi = (lane >= lo) & (lane < lo + n) & in_qk
    take_lo = (lane >= hi) & (lane < hi + n) & in_qk
    for h in range(tc // head_dim):
        seg = x[:, h * head_dim:(h + 1) * head_dim]
        seg = jnp.where(take_hi, pltpu.roll(seg, head_dim - (hi - lo), 1),
                        jnp.where(take_lo, pltpu.roll(seg, hi - lo, 1), seg))
        o_ref[:, h * head_dim:(h + 1) * head_dim] = seg.astype(BF16)


def _cast_weight(w_stack, layer, *, pad_rows_to=1, pad_cols_to=1, swap=None):
    _, rows, cols = w_stack.shape
    rp = -(-rows // pad_rows_to) * pad_rows_to
    cp = -(-cols // pad_cols_to) * pad_cols_to
    tc = _pick(cp, (2048, 1024, 512, 256, 128))
    tr = _pick(rp, tuple(t for t in (2048, 1024, 512, 256, 128, 64, 32, 16) if t * tc <= CAST_BLOCK_ELEMS))
    if swap is not None:
        assert swap[4] % tc == 0 and tc % swap[0] == 0
    kern = functools.partial(_cast_kernel, rows=rows, cols=cols, swap=swap)
    return pl.pallas_call(
        kern,
        grid=(rp // tr, cp // tc),
        in_specs=[pl.BlockSpec((None, tr, tc), lambda i, j: (layer, jnp.minimum(i, (rows - 1) // tr),
                                                              jnp.minimum(j, (cols - 1) // tc)))],
        out_specs=pl.BlockSpec((tr, tc), lambda i, j: (i, j)),
        out_shape=jax.ShapeDtypeStruct((rp, cp), BF16),
        compiler_params=_params("arbitrary", "arbitrary"),
        name="cast_weight",
    )(w_stack)


def _pad_cols(a, n):
    return a if a.shape[-1] == n else jnp.pad(a, [(0, 0)] * (a.ndim - 1) + [(0, n - a.shape[-1])])


def kernel(x, positions, norm_mix, norm_ffn, norm_final, conv_w_in, conv_b_in, conv_dw, conv_dw_b, conv_ln_g, conv_ln_b, conv_w_out, conv_b_out, attn_w_qkv, attn_lq1, attn_lk1, attn_lq2, attn_lk2, attn_subln_g, attn_w_o, ffn_w_gate, ffn_w_up, ffn_dw, ffn_dw_b, ffn_w_down):
    batch, seq_len, d = x.shape
    depth = norm_mix.shape[0]
    m = batch * seq_len
    d_ff = ffn_w_gate.shape[2]
    head_dim = attn_lq1.shape[1]
    heads = d // (2 * head_dim)
    qk_width = 2 * heads * head_dim

    tm = _pick(seq_len, (1024, 512, 256, 128, 64, 32))
    tm_proj = _pick(seq_len, (512, 256, 128, 64, 32))
    tm_conv = _pick(seq_len, (256, 128, 64))
    tn = _pick(d, (512, 256))
    tn_qkv = _pick(qk_width, (1024, 512, 256))
    tq = _pick(seq_len, (1024, 512, 256))
    tf = 2 * V7X_MXU_DIM
    f_pad = -(-d_ff // tf) * tf

    x2 = x.reshape(m, d)
    for i in range(depth):
        j = i // N_MIXERS
        if i % N_MIXERS == 0:
            glu = _conf_in(x2, norm_mix[i], _cast_weight(conv_w_in, j), conv_b_in[j], tm=tm, tn=tn)
            x2 = _conf_out(glu, x2, conv_dw[j], conv_dw_b[j], conv_ln_g[j], conv_ln_b[j],
                           _cast_weight(conv_w_out, j), conv_b_out[j], tm=tm_conv, seq_len=seq_len)
        else:
            lam_init = 0.8 - 0.6 * math.exp(-0.3 * i)
            (lo, hi, n), invf, half = _rope_layout(head_dim)
            w_qkv = _cast_weight(attn_w_qkv, j, swap=(head_dim, lo, hi, n, 2 * qk_width))
            qkv = _qkv(x2, norm_mix[i], positions, w_qkv, invf,
                       tm=tm, tn=tn_qkv, head_dim=head_dim, half=half, qk_width=qk_width)
            o = _attn(qkv, attn_lq1[j], attn_lk1[j], attn_lq2[j], attn_lk2[j], attn_subln_g[j],
                      batch=batch, seq_len=seq_len, heads=heads, head_dim=head_dim, tq=tq,
                      lam_init=lam_init)
            x2 = _proj_res(o, _cast_weight(attn_w_o, j), x2, tm=tm_proj)
        final_g = norm_final if i == depth - 1 else None
        x2 = _ffn(x2, norm_ffn[i],
                  _cast_weight(ffn_w_gate, i, pad_cols_to=tf),
                  _cast_weight(ffn_w_up, i, pad_cols_to=tf),
                  _pad_cols(ffn_dw[i], f_pad), _pad_cols(ffn_dw_b[i], f_pad),
                  _cast_weight(ffn_w_down, i, pad_rows_to=tf),
                  final_g, tm=tm, tf=tf, seq_len=seq_len)
    if depth == 0:
        raise ValueError("depth must be positive")
    return x2.reshape(batch, seq_len, d)
```

```python
import functools
import math

import jax
import jax.numpy as jnp
from jax import lax
from jax.experimental import pallas as pl
from jax.experimental.pallas import tpu as pltpu

RMS_EPS = 1e-6
LN_EPS = 1e-5
ROPE_THETA = 500000.0
N_MIXERS = 2

V7X_LANES = 128
SUBLANES = 8
V7X_MXU_DIM = 256
V7X_VMEM_LIMIT_BYTES = 60 * 1024 * 1024

F32 = jnp.float32
BF16 = jnp.bfloat16
MASK_VALUE = -1e30

CONV_HALO = 32
CONV_ROWS = 256
CONV_LANES = 128
LN_ROWS = 32
LN_UNROLL = 8
FFN_HALO = 8
CAST_BLOCK_ELEMS = 1024 * 1024


def _pick(n, candidates):
    for c in candidates:
        if n % c == 0:
            return c
    raise ValueError(f"no tile in {candidates} divides {n}")


def _params(*sem):
    return pltpu.CompilerParams(dimension_semantics=sem,
                                vmem_limit_bytes=V7X_VMEM_LIMIT_BYTES)


def _rms(x, g):
    ms = jnp.mean(x * x, axis=-1, keepdims=True)
    return x * lax.rsqrt(ms + RMS_EPS) * g


def _dot(a, b):
    return jnp.dot(a, b, preferred_element_type=F32)


def _conf_in_kernel(x_ref, ng_ref, wa_ref, wb_ref, ba_ref, bb_ref, o_ref, hn_ref):
    def glu(hn):
        for c in range(o_ref.shape[1] // V7X_MXU_DIM):
            cols = slice(c * V7X_MXU_DIM, (c + 1) * V7X_MXU_DIM)
            a = _dot(hn, wa_ref[:, cols]) + ba_ref[:, cols]
            gate = _dot(hn, wb_ref[:, cols]) + bb_ref[:, cols]
            o_ref[:, cols] = a * jax.nn.sigmoid(gate)

    @pl.when(pl.program_id(1) == 0)
    def _():
        hn = _rms(x_ref[...], ng_ref[...]).astype(BF16)
        hn_ref[...] = hn
        glu(hn)

    @pl.when(pl.program_id(1) > 0)
    def _():
        glu(hn_ref[...])


def _conf_in(x2, norm_g, w_in, b_in, *, tm, tn):
    m, d = x2.shape
    nj = d // tn
    return pl.pallas_call(
        _conf_in_kernel,
        grid=(m // tm, nj),
        in_specs=[
            pl.BlockSpec((tm, d), lambda i, j: (i, 0)),
            pl.BlockSpec((1, d), lambda i, j: (0, 0)),
            pl.BlockSpec((d, tn), lambda i, j: (0, j)),
            pl.BlockSpec((d, tn), lambda i, j: (0, j + nj)),
            pl.BlockSpec((1, tn), lambda i, j: (0, j)),
            pl.BlockSpec((1, tn), lambda i, j: (0, j + nj)),
        ],
        out_specs=pl.BlockSpec((tm, tn), lambda i, j: (i, j)),
        out_shape=jax.ShapeDtypeStruct((m, d), F32),
        scratch_shapes=[pltpu.VMEM((tm, d), BF16)],
        compiler_params=_params("arbitrary", "arbitrary"),
        name="conf_in",
    )(x2, norm_g.reshape(1, d), w_in, w_in, b_in.reshape(1, 2 * d), b_in.reshape(1, 2 * d))


def _conf_out_kernel(glu_ref, halo_ref, x_ref, dw_ref, dwb_ref, lng_ref, lnb_ref,
                     wo_ref, bo_ref, o_ref, cbuf_ref, xs_ref, wb_ref, h_ref, y_ref, *, tm, seq_len, width):
    d = glu_ref.shape[1]
    seq_start = (pl.program_id(0) * tm) % seq_len == 0

    @pl.when(pl.program_id(0) == 0)
    def _():
        for k in range(width):
            wb_ref[k] = jnp.broadcast_to(dw_ref[k:k + 1, :], (SUBLANES, d))

    @pl.when(seq_start)
    def _():
        cbuf_ref[0:CONV_HALO, :] = jnp.zeros((CONV_HALO, d), F32)

    @pl.when(jnp.logical_not(seq_start))
    def _():
        cbuf_ref[0:CONV_HALO, :] = halo_ref[...]

    cbuf_ref[CONV_HALO:, :] = glu_ref[...]

    base = CONV_HALO - (width - 1)
    conv_rows_blk = math.gcd(tm, CONV_ROWS)

    def lane_block(c, carry):
        lanes = pl.ds(pl.multiple_of(c * CONV_LANES, CONV_LANES), CONV_LANES)
        for s in range(min(SUBLANES, width)):
            span = tm + SUBLANES * ((width - 1 - s) // SUBLANES)
            xs_ref[s, 0:span, :] = cbuf_ref[base + s:base + s + span, lanes]
        def conv_rows(rb, carry2):
            r0 = pl.multiple_of(rb * conv_rows_blk, conv_rows_blk)
            groups = conv_rows_blk // SUBLANES
            accs = [jnp.zeros((SUBLANES, CONV_LANES), F32)] * groups
            for k in range(width):
                a, s = divmod(k, SUBLANES)
                w8 = wb_ref[k, :, lanes]
                for g in range(groups):
                    accs[g] = accs[g] + w8 * xs_ref[s, pl.ds(r0 + SUBLANES * (a + g), SUBLANES), :]
            bias = dwb_ref[:, lanes]
            for g in range(groups):
                h_ref[pl.ds(r0 + SUBLANES * g, SUBLANES), lanes] = accs[g] + bias
            return carry2

        lax.fori_loop(0, tm // conv_rows_blk, conv_rows, 0)
        return carry

    lax.fori_loop(0, d // CONV_LANES, lane_block, 0)

    def row_block(rb, carry):
        r0 = pl.multiple_of(rb * LN_ROWS, LN_ROWS)
        h = h_ref[pl.ds(r0, LN_ROWS), :]
        mu = jnp.mean(h, axis=-1, keepdims=True)
        hc = h - mu
        var = jnp.mean(hc * hc, axis=-1, keepdims=True)
        y = hc * lax.rsqrt(var + LN_EPS) * lng_ref[...] + lnb_ref[...]
        y_ref[pl.ds(r0, LN_ROWS), :] = (y * jax.nn.sigmoid(y)).astype(BF16)
        return carry

    trips = tm // LN_ROWS
    lax.fori_loop(0, trips, row_block, 0, unroll=math.gcd(trips, LN_UNROLL))
    o_ref[...] = x_ref[...] + _dot(y_ref[...], wo_ref[...]) + bo_ref[...]


def _conf_out(glu, x2, dw, dwb, ln_g, ln_b, w_out, b_out, *, tm, seq_len):
    m, d = x2.shape
    width = dw.shape[0]
    assert width - 1 <= CONV_HALO and tm % CONV_HALO == 0 and tm % LN_ROWS == 0
    assert d % CONV_LANES == 0
    hb = tm // CONV_HALO
    row = lambda a: a.reshape(1, d)
    kern = functools.partial(_conf_out_kernel, tm=tm, seq_len=seq_len, width=width)
    return pl.pallas_call(
        kern,
        grid=(m // tm,),
        in_specs=[
            pl.BlockSpec((tm, d), lambda i: (i, 0)),
            pl.BlockSpec((CONV_HALO, d), lambda i: (jnp.maximum(i * hb - 1, 0), 0)),
            pl.BlockSpec((tm, d), lambda i: (i, 0)),
            pl.BlockSpec((width, d), lambda i: (0, 0)),
            pl.BlockSpec((1, d), lambda i: (0, 0)),
            pl.BlockSpec((1, d), lambda i: (0, 0)),
            pl.BlockSpec((1, d), lambda i: (0, 0)),
            pl.BlockSpec((d, d), lambda i: (0, 0)),
            pl.BlockSpec((1, d), lambda i: (0, 0)),
        ],
        out_specs=pl.BlockSpec((tm, d), lambda i: (i, 0)),
        out_shape=jax.ShapeDtypeStruct((m, d), F32),
        scratch_shapes=[pltpu.VMEM((CONV_HALO + tm, d), F32),
                        pltpu.VMEM((SUBLANES, CONV_HALO + tm, CONV_LANES), F32),
                        pltpu.VMEM((width, SUBLANES, d), F32),
                        pltpu.VMEM((tm, d), F32),
                        pltpu.VMEM((tm, d), BF16)],
        compiler_params=_params("arbitrary"),
        name="conf_out",
    )(glu, glu, x2, dw, row(dwb), row(ln_g), row(ln_b), w_out, row(b_out))


def _ffn_kernel(x_ref, ng_ref, wg_ref, wu_ref, dw_ref, dwb_ref, wd_ref, fg_ref, o_ref,
                hn_ref, gbuf_ref, carry_ref, *, tm, seq_len, final_norm):
    i = pl.program_id(0)
    j = pl.program_id(1)

    tf = wg_ref.shape[1]
    seq_start = (i * tm) % seq_len == 0

    @pl.when(seq_start)
    def _():
        gbuf_ref[0:FFN_HALO, :] = jnp.zeros((FFN_HALO, tf), F32)

    @pl.when(jnp.logical_not(seq_start))
    def _():
        gbuf_ref[0:FFN_HALO, :] = carry_ref[j]

    def hidden(hn):
        down = None
        for c in range(tf // V7X_MXU_DIM):
            cols = slice(c * V7X_MXU_DIM, (c + 1) * V7X_MXU_DIM)
            up = _dot(hn, wu_ref[:, cols])
            g_lin = _dot(hn, wg_ref[:, cols])
            gbuf_ref[FFN_HALO:, cols] = g_lin
            carry_ref[j, :, cols] = g_lin[tm - FFN_HALO:, :]
            g = (dw_ref[2:3, cols] * g_lin
                 + dw_ref[1:2, cols] * gbuf_ref[FFN_HALO - 1:FFN_HALO - 1 + tm, cols]
                 + dw_ref[0:1, cols] * gbuf_ref[FFN_HALO - 2:FFN_HALO - 2 + tm, cols]
                 + dwb_ref[:, cols])
            act = (g * jax.nn.sigmoid(g) * up).astype(BF16)
            part = _dot(act, wd_ref[cols, :])
            down = part if down is None else down + part
        return down

    @pl.when(j == 0)
    def _():
        x = x_ref[...]
        hn = _rms(x, ng_ref[...]).astype(BF16)
        hn_ref[...] = hn
        o_ref[...] = x + hidden(hn)

    @pl.when(j > 0)
    def _():
        o_ref[...] += hidden(hn_ref[...])

    if final_norm:
        @pl.when(j == pl.num_programs(1) - 1)
        def _():
            o_ref[...] = _rms(o_ref[...], fg_ref[...])


def _ffn(x2, norm_g, w_gate, w_up, dw, dwb, w_down, final_g, *, tm, tf, seq_len):
    m, d = x2.shape
    fp = w_gate.shape[1]
    assert dw.shape[0] == 3 and fp % tf == 0
    nj = fp // tf
    final_norm = final_g is not None
    fg = (final_g if final_norm else jnp.ones((d,), F32)).reshape(1, d)
    kern = functools.partial(_ffn_kernel, tm=tm, seq_len=seq_len, final_norm=final_norm)
    return pl.pallas_call(
        kern,
        grid=(m // tm, nj),
        in_specs=[
            pl.BlockSpec((tm, d), lambda i, j: (i, 0)),
            pl.BlockSpec((1, d), lambda i, j: (0, 0)),
            pl.BlockSpec((d, tf), lambda i, j: (0, j)),
            pl.BlockSpec((d, tf), lambda i, j: (0, j)),
            pl.BlockSpec((3, tf), lambda i, j: (0, j)),
            pl.BlockSpec((1, tf), lambda i, j: (0, j)),
            pl.BlockSpec((tf, d), lambda i, j: (j, 0)),
            pl.BlockSpec((1, d), lambda i, j: (0, 0)),
        ],
        out_specs=pl.BlockSpec((tm, d), lambda i, j: (i, 0)),
        out_shape=jax.ShapeDtypeStruct((m, d), F32),
        scratch_shapes=[
            pltpu.VMEM((tm, d), BF16),
            pltpu.VMEM((FFN_HALO + tm, tf), F32),
            pltpu.VMEM((nj, FFN_HALO, tf), F32),
        ],
        compiler_params=_params("arbitrary", "arbitrary"),
        name="ffn",
    )(x2, norm_g.reshape(1, d), w_gate, w_up, dw, dwb.reshape(1, fp), w_down, fg)


def _qkv_kernel(x_ref, ng_ref, pos_ref, invf_ref, w_ref, o_ref, hn_ref, cos_ref, sin_ref,
                *, tiles_per_kind, head_dim, half, q_scale):
    j = pl.program_id(1)
    tn = o_ref.shape[1]
    pair = head_dim // 2

    def project(hn, cos, sin):
        for c in range(tn // V7X_MXU_DIM):
            acc = _dot(hn, w_ref[:, c * V7X_MXU_DIM:(c + 1) * V7X_MXU_DIM])
            for h in range(V7X_MXU_DIM // head_dim):
                seg = acc[:, h * head_dim:(h + 1) * head_dim]
                col = c * V7X_MXU_DIM + h * head_dim
                o_ref[:, col:col + head_dim] = (seg * cos + pltpu.roll(seg, pair, 1) * sin).astype(BF16)

    @pl.when(j == 0)
    def _():
        hn = _rms(x_ref[...], ng_ref[...]).astype(BF16)
        hn_ref[...] = hn
        ang = pos_ref[...].astype(F32) * invf_ref[...]
        lane = lax.broadcasted_iota(jnp.int32, ang.shape, 1)
        cos = jnp.cos(ang)
        sin = jnp.where(lane < half, -jnp.sin(ang), jnp.sin(ang))
        cos_ref[0] = cos * q_scale
        sin_ref[0] = sin * q_scale
        cos_ref[1] = cos
        sin_ref[1] = sin
        cos_ref[2] = jnp.ones_like(cos)
        sin_ref[2] = jnp.zeros_like(sin)
        project(hn, cos * q_scale, sin * q_scale)

    @pl.when(j > 0)
    def _():
        kind = jnp.minimum(j // tiles_per_kind, 2)
        project(hn_ref[...], cos_ref[kind], sin_ref[kind])


def _rope_layout(head_dim):
    rot = head_dim // 4
    half = rot // 2
    pair = head_dim // 2
    inv_freq = ROPE_THETA ** (-jnp.arange(half, dtype=F32) / half)
    invf = jnp.zeros((head_dim,), F32).at[:half].set(inv_freq).at[pair:pair + half].set(inv_freq)
    return (half, pair, half), invf.reshape(1, head_dim), half


def _qkv(x2, norm_g, positions, w_qkv, invf, *, tm, tn, head_dim, half, qk_width):
    m, d = x2.shape
    n = w_qkv.shape[1]
    assert head_dim == V7X_LANES and qk_width % tn == 0 and n % tn == 0 and tn % V7X_MXU_DIM == 0
    kern = functools.partial(_qkv_kernel, tiles_per_kind=qk_width // tn, head_dim=head_dim, half=half,
                             q_scale=head_dim ** -0.5 * math.log2(math.e))
    return pl.pallas_call(
        kern,
        grid=(m // tm, n // tn),
        in_specs=[
            pl.BlockSpec((tm, d), lambda i, j: (i, 0)),
            pl.BlockSpec((1, d), lambda i, j: (0, 0)),
            pl.BlockSpec((tm, 1), lambda i, j: (i, 0)),
            pl.BlockSpec((1, head_dim), lambda i, j: (0, 0)),
            pl.BlockSpec((d, tn), lambda i, j: (0, j)),
        ],
        out_specs=pl.BlockSpec((tm, tn), lambda i, j: (i, j)),
        out_shape=jax.ShapeDtypeStruct((m, n), BF16),
        scratch_shapes=[pltpu.VMEM((tm, d), BF16)] + [pltpu.VMEM((3, tm, head_dim), F32)] * 2,
        compiler_params=_params("arbitrary", "arbitrary"),
        name="qkv",
    )(x2, norm_g.reshape(1, d), positions.reshape(m, 1), invf, w_qkv)


def _attn_kernel(q_ref, k_ref, v_ref, lq1_ref, lk1_ref, lq2_ref, lk2_ref, sg_ref, o_ref,
                 m_ref, l_ref, acc_ref, sa_ref, sb_ref, ma_ref, mb_ref, *, tq, head_dim, lam_init):
    qi = pl.program_id(2)
    vd = 2 * head_dim
    m_ref[...] = jnp.full(m_ref.shape, MASK_VALUE, F32)
    l_ref[...] = jnp.zeros(l_ref.shape, F32)
    acc_ref[...] = jnp.zeros(acc_ref.shape, F32)

    def scores(kj, buf):
        s_ref, mt_ref = buf
        k0 = pl.multiple_of(kj * tq, tq)
        for idx in range(2):
            cols = slice(idx * head_dim, (idx + 1) * head_dim)
            s = lax.dot_general(q_ref[:, cols], k_ref[pl.ds(k0, tq), cols],
                                (((1,), (1,)), ((), ())), preferred_element_type=F32)
            s_ref[idx] = s
            mt_ref[idx] = jnp.broadcast_to(jnp.max(s, axis=1, keepdims=True), (tq, V7X_LANES))

    def fold(kj, buf, masked=False):
        s_ref, mt_ref = buf
        v = v_ref[pl.ds(pl.multiple_of(kj * tq, tq), tq), :]
        for idx in range(2):
            s = s_ref[idx]
            if masked:
                row = lax.broadcasted_iota(jnp.int32, s.shape, 0)
                col = lax.broadcasted_iota(jnp.int32, s.shape, 1)
                s = jnp.where(col <= row, s, MASK_VALUE)
                m_tile = jnp.max(s, axis=1, keepdims=True)
            else:
                m_tile = mt_ref[idx]
            m_prev = m_ref[idx]
            m_new = jnp.maximum(m_prev, m_tile)
            alpha = jnp.exp2(m_prev - m_new)
            ps = [jnp.exp2(s[:, c * V7X_LANES:(c + 1) * V7X_LANES] - m_new)
                  for c in range(tq // V7X_LANES)]
            l_ref[idx] = alpha * l_ref[idx] + functools.reduce(jnp.add, ps)
            pv = _dot(jnp.concatenate(ps, axis=1).astype(BF16), v)
            alpha_v = jnp.concatenate([alpha] * (vd // V7X_LANES), axis=1)
            acc_ref[idx] = alpha_v * acc_ref[idx] + pv
            m_ref[idx] = m_new

    buf_a = (sa_ref, ma_ref)
    buf_b = (sb_ref, mb_ref)
    scores(0, buf_a)

    def pair(p, carry):
        scores(2 * p + 1, buf_b)
        fold(2 * p, buf_a)
        scores(2 * p + 2, buf_a)
        fold(2 * p + 1, buf_b)
        return carry

    lax.fori_loop(0, lax.shift_right_logical(qi, jnp.int32(1)), pair, 0)

    @pl.when((qi & 1) == 0)
    def _():
        fold(qi, buf_a, masked=True)

    @pl.when((qi & 1) == 1)
    def _():
        scores(qi, buf_b)
        fold(qi - 1, buf_a)
        fold(qi, buf_b, masked=True)

    lam = (jnp.exp(jnp.sum(lq1_ref[...] * lk1_ref[...], keepdims=True))
           - jnp.exp(jnp.sum(lq2_ref[...] * lk2_ref[...], keepdims=True)) + lam_init)
    l0 = jnp.sum(l_ref[0], axis=1, keepdims=True)
    l1 = jnp.sum(l_ref[1], axis=1, keepdims=True)
    o = acc_ref[0] / l0 - lam * (acc_ref[1] / l1)
    o_ref[...] = (_rms(o, sg_ref[...]) * (1.0 - lam_init)).astype(BF16)


def _attn(qkv, lq1, lk1, lq2, lk2, subln_g, *, batch, seq_len, heads, head_dim, tq, lam_init):
    m = qkv.shape[0]
    vd = 2 * head_dim
    nq = seq_len // tq
    k_col0 = heads
    v_col0 = 2 * heads
    vec = lambda a: a.reshape(1, head_dim)
    kern = functools.partial(_attn_kernel, tq=tq, head_dim=head_dim, lam_init=lam_init)
    small = pl.BlockSpec((1, head_dim), lambda b, h, q: (0, 0))
    return pl.pallas_call(
        kern,
        grid=(batch, heads, nq),
        in_specs=[
            pl.BlockSpec((tq, vd), lambda b, h, q: (b * nq + q, h)),
            pl.BlockSpec((seq_len, vd), lambda b, h, q: (b, k_col0 + h)),
            pl.BlockSpec((seq_len, vd), lambda b, h, q: (b, v_col0 + h)),
            small, small, small, small,
            pl.BlockSpec((1, vd), lambda b, h, q: (0, 0)),
        ],
        out_specs=pl.BlockSpec((tq, vd), lambda b, h, q: (b * nq + q, h)),
        out_shape=jax.ShapeDtypeStruct((m, heads * vd), BF16),
        scratch_shapes=[
            pltpu.VMEM((2, tq, V7X_LANES), F32),
            pltpu.VMEM((2, tq, V7X_LANES), F32),
            pltpu.VMEM((2, tq, vd), F32),
            pltpu.VMEM((2, tq, tq), F32),
            pltpu.VMEM((2, tq, tq), F32),
            pltpu.VMEM((2, tq, V7X_LANES), F32),
            pltpu.VMEM((2, tq, V7X_LANES), F32),
        ],
        compiler_params=_params("arbitrary", "arbitrary", "arbitrary"),
        name="diff_attn",
    )(qkv, qkv, qkv, vec(lq1), vec(lk1), vec(lq2), vec(lk2), subln_g.reshape(1, vd))


def _proj_res_kernel(a_ref, w_ref, r_ref, o_ref):
    o_ref[...] = r_ref[...] + _dot(a_ref[...], w_ref[...])


def _proj_res(a, w, res, *, tm):
    m, k = a.shape
    n = w.shape[1]
    return pl.pallas_call(
        _proj_res_kernel,
        grid=(m // tm,),
        in_specs=[
            pl.BlockSpec((tm, k), lambda i: (i, 0)),
            pl.BlockSpec((k, n), lambda i: (0, 0)),
            pl.BlockSpec((tm, n), lambda i: (i, 0)),
        ],
        out_specs=pl.BlockSpec((tm, n), lambda i: (i, 0)),
        out_shape=jax.ShapeDtypeStruct((m, n), F32),
        compiler_params=_params("arbitrary"),
        name="attn_out",
    )(a, w, res)


def _cast_kernel(w_ref, o_ref, *, rows, cols, swap):
    tr, tc = o_ref.shape
    x = w_ref[...]
    if rows % tr or cols % tc:
        r = pl.program_id(0) * tr + lax.broadcasted_iota(jnp.int32, x.shape, 0)
        c = pl.program_id(1) * tc + lax.broadcasted_iota(jnp.int32, x.shape, 1)
        x = jnp.where((r < rows) & (c < cols), x, 0.0)
    if swap is None:
        o_ref[...] = x.astype(BF16)
        return
    head_dim, lo, hi, n, limit = swap
    in_qk = pl.program_id(1) * tc < limit
    lane = lax.broadcasted_iota(jnp.int32, (tr, head_dim), 1)
    take_hi = (lane >= lo) & (lane < lo + n) & in_qk
    take_lo = (lane >= hi) & (lane < hi + n) & in_qk
    for h in range(tc // head_dim):
        seg = x[:, h * head_dim:(h + 1) * head_dim]
        seg = jnp.where(take_hi, pltpu.roll(seg, head_dim - (hi - lo), 1),
                        jnp.where(take_lo, pltpu.roll(seg, hi - lo, 1), seg))
        o_ref[:, h * head_dim:(h + 1) * head_dim] = seg.astype(BF16)


def _cast_weight(w_stack, layer, *, pad_rows_to=1, pad_cols_to=1, swap=None):
    _, rows, cols = w_stack.shape
    rp = -(-rows // pad_rows_to) * pad_rows_to
    cp = -(-cols // pad_cols_to) * pad_cols_to
    tc = _pick(cp, (2048, 1024, 512, 256, 128))
    tr = _pick(rp, tuple(t for t in (2048, 1024, 512, 256, 128, 64, 32, 16) if t * tc <= CAST_BLOCK_ELEMS))
    if swap is not None:
        assert swap[4] % tc == 0 and tc % swap[0] == 0
    kern = functools.partial(_cast_kernel, rows=rows, cols=cols, swap=swap)
    return pl.pallas_call(
        kern,
        grid=(rp // tr, cp // tc),
        in_specs=[pl.BlockSpec((None, tr, tc), lambda i, j: (layer, jnp.minimum(i, (rows - 1) // tr),
                                                              jnp.minimum(j, (cols - 1) // tc)))],
        out_specs=pl.BlockSpec((tr, tc), lambda i, j: (i, j)),
        out_shape=jax.ShapeDtypeStruct((rp, cp), BF16),
        compiler_params=_params("arbitrary", "arbitrary"),
        name="cast_weight",
    )(w_stack)


def _pad_cols(a, n):
    return a if a.shape[-1] == n else jnp.pad(a, [(0, 0)] * (a.ndim - 1) + [(0, n - a.shape[-1])])


def kernel(x, positions, norm_mix, norm_ffn, norm_final, conv_w_in, conv_b_in, conv_dw, conv_dw_b, conv_ln_g, conv_ln_b, conv_w_out, conv_b_out, attn_w_qkv, attn_lq1, attn_lk1, attn_lq2, attn_lk2, attn_subln_g, attn_w_o, ffn_w_gate, ffn_w_up, ffn_dw, ffn_dw_b, ffn_w_down):
    batch, seq_len, d = x.shape
    depth = norm_mix.shape[0]
    m = batch * seq_len
    d_ff = ffn_w_gate.shape[2]
    head_dim = attn_lq1.shape[1]
    heads = d // (2 * head_dim)
    qk_width = 2 * heads * head_dim

    tm = _pick(seq_len, (1024, 512, 256, 128, 64, 32))
    tm_proj = _pick(seq_len, (1024, 512, 256, 128, 64, 32))
    tm_conv = _pick(seq_len, (256, 128, 64))
    tn = _pick(d, (512, 256))
    tn_qkv = _pick(qk_width, (1024, 512, 256))
    tq = _pick(seq_len, (1024, 512, 256))
    tf = 2 * V7X_MXU_DIM
    f_pad = -(-d_ff // tf) * tf

    x2 = x.reshape(m, d)
    for i in range(depth):
        j = i // N_MIXERS
        if i % N_MIXERS == 0:
            glu = _conf_in(x2, norm_mix[i], _cast_weight(conv_w_in, j), conv_b_in[j], tm=tm, tn=tn)
            x2 = _conf_out(glu, x2, conv_dw[j], conv_dw_b[j], conv_ln_g[j], conv_ln_b[j],
                           _cast_weight(conv_w_out, j), conv_b_out[j], tm=tm_conv, seq_len=seq_len)
        else:
            lam_init = 0.8 - 0.6 * math.exp(-0.3 * i)
            (lo, hi, n), invf, half = _rope_layout(head_dim)
            w_qkv = _cast_weight(attn_w_qkv, j, swap=(head_dim, lo, hi, n, 2 * qk_width))
            qkv = _qkv(x2, norm_mix[i], positions, w_qkv, invf,
                       tm=tm, tn=tn_qkv, head_dim=head_dim, half=half, qk_width=qk_width)
            o = _attn(qkv, attn_lq1[j], attn_lk1[j], attn_lq2[j], attn_lk2[j], attn_subln_g[j],
                      batch=batch, seq_len=seq_len, heads=heads, head_dim=head_dim, tq=tq,
                      lam_init=lam_init)
            x2 = _proj_res(o, _cast_weight(attn_w_o, j), x2, tm=tm_proj)
        final_g = norm_final if i == depth - 1 else None
        x2 = _ffn(x2, norm_ffn[i],
                  _cast_weight(ffn_w_gate, i, pad_cols_to=tf),
                  _cast_weight(ffn_w_up, i, pad_cols_to=tf),
                  _pad_cols(ffn_dw[i], f_pad), _pad_cols(ffn_dw_b[i], f_pad),
                  _cast_weight(ffn_w_down, i, pad_rows_to=tf),
                  final_g, tm=tm, tf=tf, seq_len=seq_len)
    if depth == 0:
        raise ValueError("depth must be positive")
    return x2.reshape(batch, seq_len, d)
```

```python
import functools
import math

import jax
import jax.numpy as jnp
from jax import lax
from jax.experimental import pallas as pl
from jax.experimental.pallas import tpu as pltpu

RMS_EPS = 1e-6
LN_EPS = 1e-5
ROPE_THETA = 500000.0
N_MIXERS = 2

V7X_LANES = 128
SUBLANES = 8
V7X_MXU_DIM = 256
V7X_VMEM_LIMIT_BYTES = 60 * 1024 * 1024

F32 = jnp.float32
BF16 = jnp.bfloat16
MASK_VALUE = -1e30

CONV_HALO = 32
CONV_ROWS = 256
CONV_LANES = 128
LN_ROWS = 32
LN_UNROLL = 8
FFN_HALO = 8
CAST_BLOCK_ELEMS = 1024 * 1024


def _pick(n, candidates):
    for c in candidates:
        if n % c == 0:
            return c
    raise ValueError(f"no tile in {candidates} divides {n}")


def _params(*sem):
    return pltpu.CompilerParams(dimension_semantics=sem,
                                vmem_limit_bytes=V7X_VMEM_LIMIT_BYTES)


def _rms(x, g):
    ms = jnp.mean(x * x, axis=-1, keepdims=True)
    return x * lax.rsqrt(ms + RMS_EPS) * g


def _dot(a, b):
    return jnp.dot(a, b, preferred_element_type=F32)


def _conf_in_kernel(x_ref, ng_ref, wa_ref, wb_ref, ba_ref, bb_ref, o_ref, hn_ref):
    def glu(hn):
        for c in range(o_ref.shape[1] // V7X_MXU_DIM):
            cols = slice(c * V7X_MXU_DIM, (c + 1) * V7X_MXU_DIM)
            a = _dot(hn, wa_ref[:, cols]) + ba_ref[:, cols]
            gate = _dot(hn, wb_ref[:, cols]) + bb_ref[:, cols]
            o_ref[:, cols] = a * jax.nn.sigmoid(gate)

    @pl.when(pl.program_id(1) == 0)
    def _():
        hn = _rms(x_ref[...], ng_ref[...]).astype(BF16)
        hn_ref[...] = hn
        glu(hn)

    @pl.when(pl.program_id(1) > 0)
    def _():
        glu(hn_ref[...])


def _conf_in(x2, norm_g, w_in, b_in, *, tm, tn):
    m, d = x2.shape
    nj = d // tn
    return pl.pallas_call(
        _conf_in_kernel,
        grid=(m // tm, nj),
        in_specs=[
            pl.BlockSpec((tm, d), lambda i, j: (i, 0)),
            pl.BlockSpec((1, d), lambda i, j: (0, 0)),
            pl.BlockSpec((d, tn), lambda i, j: (0, j)),
            pl.BlockSpec((d, tn), lambda i, j: (0, j + nj)),
            pl.BlockSpec((1, tn), lambda i, j: (0, j)),
            pl.BlockSpec((1, tn), lambda i, j: (0, j + nj)),
        ],
        out_specs=pl.BlockSpec((tm, tn), lambda i, j: (i, j)),
        out_shape=jax.ShapeDtypeStruct((m, d), F32),
        scratch_shapes=[pltpu.VMEM((tm, d), BF16)],
        compiler_params=_params("arbitrary", "arbitrary"),
        name="conf_in",
    )(x2, norm_g.reshape(1, d), w_in, w_in, b_in.reshape(1, 2 * d), b_in.reshape(1, 2 * d))


def _conf_out_kernel(glu_ref, halo_ref, x_ref, dw_ref, dwb_ref, lng_ref, lnb_ref,
                     wo_ref, bo_ref, o_ref, cbuf_ref, xs_ref, wb_ref, h_ref, y_ref, *, tm, seq_len, width):
    d = glu_ref.shape[1]
    seq_start = (pl.program_id(0) * tm) % seq_len == 0

    @pl.when(pl.program_id(0) == 0)
    def _():
        for k in range(width):
            wb_ref[k] = jnp.broadcast_to(dw_ref[k:k + 1, :], (SUBLANES, d))

    @pl.when(seq_start)
    def _():
        cbuf_ref[0:CONV_HALO, :] = jnp.zeros((CONV_HALO, d), F32)

    @pl.when(jnp.logical_not(seq_start))
    def _():
        cbuf_ref[0:CONV_HALO, :] = halo_ref[...]

    cbuf_ref[CONV_HALO:, :] = glu_ref[...]

    base = CONV_HALO - (width - 1)
    conv_rows_blk = math.gcd(tm, CONV_ROWS)

    def lane_block(c, carry):
        lanes = pl.ds(pl.multiple_of(c * CONV_LANES, CONV_LANES), CONV_LANES)
        for s in range(min(SUBLANES, width)):
            span = tm + SUBLANES * ((width - 1 - s) // SUBLANES)
            xs_ref[s, 0:span, :] = cbuf_ref[base + s:base + s + span, lanes]
        def conv_rows(rb, carry2):
            r0 = pl.multiple_of(rb * conv_rows_blk, conv_rows_blk)
            groups = conv_rows_blk // SUBLANES
            accs = [jnp.zeros((SUBLANES, CONV_LANES), F32)] * groups
            for k in sorted(range(width), key=lambda t: (t % SUBLANES, t // SUBLANES)):
                a, s = divmod(k, SUBLANES)
                w8 = wb_ref[k, :, lanes]
                for g in range(groups):
                    accs[g] = accs[g] + w8 * xs_ref[s, pl.ds(r0 + SUBLANES * (a + g), SUBLANES), :]
            bias = dwb_ref[:, lanes]
            for g in range(groups):
                h_ref[pl.ds(r0 + SUBLANES * g, SUBLANES), lanes] = accs[g] + bias
            return carry2

        lax.fori_loop(0, tm // conv_rows_blk, conv_rows, 0)
        return carry

    lax.fori_loop(0, d // CONV_LANES, lane_block, 0)

    def row_block(rb, carry):
        r0 = pl.multiple_of(rb * LN_ROWS, LN_ROWS)
        h = h_ref[pl.ds(r0, LN_ROWS), :]
        mu = jnp.mean(h, axis=-1, keepdims=True)
        hc = h - mu
        var = jnp.mean(hc * hc, axis=-1, keepdims=True)
        y = hc * lax.rsqrt(var + LN_EPS) * lng_ref[...] + lnb_ref[...]
        y_ref[pl.ds(r0, LN_ROWS), :] = (y * jax.nn.sigmoid(y)).astype(BF16)
        return carry

    trips = tm // LN_ROWS
    lax.fori_loop(0, trips, row_block, 0, unroll=math.gcd(trips, LN_UNROLL))
    o_ref[...] = x_ref[...] + _dot(y_ref[...], wo_ref[...]) + bo_ref[...]


def _conf_out(glu, x2, dw, dwb, ln_g, ln_b, w_out, b_out, *, tm, seq_len):
    m, d = x2.shape
    width = dw.shape[0]
    assert width - 1 <= CONV_HALO and tm % CONV_HALO == 0 and tm % LN_ROWS == 0
    assert d % CONV_LANES == 0
    hb = tm // CONV_HALO
    row = lambda a: a.reshape(1, d)
    kern = functools.partial(_conf_out_kernel, tm=tm, seq_len=seq_len, width=width)
    return pl.pallas_call(
        kern,
        grid=(m // tm,),
        in_specs=[
            pl.BlockSpec((tm, d), lambda i: (i, 0)),
            pl.BlockSpec((CONV_HALO, d), lambda i: (jnp.maximum(i * hb - 1, 0), 0)),
            pl.BlockSpec((tm, d), lambda i: (i, 0)),
            pl.BlockSpec((width, d), lambda i: (0, 0)),
            pl.BlockSpec((1, d), lambda i: (0, 0)),
            pl.BlockSpec((1, d), lambda i: (0, 0)),
            pl.BlockSpec((1, d), lambda i: (0, 0)),
            pl.BlockSpec((d, d), lambda i: (0, 0)),
            pl.BlockSpec((1, d), lambda i: (0, 0)),
        ],
        out_specs=pl.BlockSpec((tm, d), lambda i: (i, 0)),
        out_shape=jax.ShapeDtypeStruct((m, d), F32),
        scratch_shapes=[pltpu.VMEM((CONV_HALO + tm, d), F32),
                        pltpu.VMEM((SUBLANES, CONV_HALO + tm, CONV_LANES), F32),
                        pltpu.VMEM((width, SUBLANES, d), F32),
                        pltpu.VMEM((tm, d), F32),
                        pltpu.VMEM((tm, d), BF16)],
        compiler_params=_params("arbitrary"),
        name="conf_out",
    )(glu, glu, x2, dw, row(dwb), row(ln_g), row(ln_b), w_out, row(b_out))


def _ffn_kernel(x_ref, ng_ref, wg_ref, wu_ref, dw_ref, dwb_ref, wd_ref, fg_ref, o_ref,
                hn_ref, gbuf_ref, carry_ref, *, tm, seq_len, final_norm):
    i = pl.program_id(0)
    j = pl.program_id(1)

    tf = wg_ref.shape[1]
    seq_start = (i * tm) % seq_len == 0

    @pl.when(seq_start)
    def _():
        gbuf_ref[0:FFN_HALO, :] = jnp.zeros((FFN_HALO, tf), F32)

    @pl.when(jnp.logical_not(seq_start))
    def _():
        gbuf_ref[0:FFN_HALO, :] = carry_ref[j]

    def hidden(hn):
        down = None
        for c in range(tf // V7X_MXU_DIM):
            cols = slice(c * V7X_MXU_DIM, (c + 1) * V7X_MXU_DIM)
            up = _dot(hn, wu_ref[:, cols])
            g_lin = _dot(hn, wg_ref[:, cols])
            gbuf_ref[FFN_HALO:, cols] = g_lin
            carry_ref[j, :, cols] = g_lin[tm - FFN_HALO:, :]
            g = (dw_ref[2:3, cols] * g_lin
                 + dw_ref[1:2, cols] * gbuf_ref[FFN_HALO - 1:FFN_HALO - 1 + tm, cols]
                 + dw_ref[0:1, cols] * gbuf_ref[FFN_HALO - 2:FFN_HALO - 2 + tm, cols]
                 + dwb_ref[:, cols])
            act = (g * jax.nn.sigmoid(g) * up).astype(BF16)
            part = _dot(act, wd_ref[cols, :])
            down = part if down is None else down + part
        return down

    @pl.when(j == 0)
    def _():
        x = x_ref[...]
        hn = _rms(x, ng_ref[...]).astype(BF16)
        hn_ref[...] = hn
        o_ref[...] = x + hidden(hn)

    @pl.when(j > 0)
    def _():
        o_ref[...] += hidden(hn_ref[...])

    if final_norm:
        @pl.when(j == pl.num_programs(1) - 1)
        def _():
            o_ref[...] = _rms(o_ref[...], fg_ref[...])


def _ffn(x2, norm_g, w_gate, w_up, dw, dwb, w_down, final_g, *, tm, tf, seq_len):
    m, d = x2.shape
    fp = w_gate.shape[1]
    assert dw.shape[0] == 3 and fp % tf == 0
    nj = fp // tf
    final_norm = final_g is not None
    fg = (final_g if final_norm else jnp.ones((d,), F32)).reshape(1, d)
    kern = functools.partial(_ffn_kernel, tm=tm, seq_len=seq_len, final_norm=final_norm)
    return pl.pallas_call(
        kern,
        grid=(m // tm, nj),
        in_specs=[
            pl.BlockSpec((tm, d), lambda i, j: (i, 0)),
            pl.BlockSpec((1, d), lambda i, j: (0, 0)),
            pl.BlockSpec((d, tf), lambda i, j: (0, j)),
            pl.BlockSpec((d, tf), lambda i, j: (0, j)),
            pl.BlockSpec((3, tf), lambda i, j: (0, j)),
            pl.BlockSpec((1, tf), lambda i, j: (0, j)),
            pl.BlockSpec((tf, d), lambda i, j: (j, 0)),
            pl.BlockSpec((1, d), lambda i, j: (0, 0)),
        ],
        out_specs=pl.BlockSpec((tm, d), lambda i, j: (i, 0)),
        out_shape=jax.ShapeDtypeStruct((m, d), F32),
        scratch_shapes=[
            pltpu.VMEM((tm, d), BF16),
            pltpu.VMEM((FFN_HALO + tm, tf), F32),
            pltpu.VMEM((nj, FFN_HALO, tf), F32),
        ],
        compiler_params=_params("arbitrary", "arbitrary"),
        name="ffn",
    )(x2, norm_g.reshape(1, d), w_gate, w_up, dw, dwb.reshape(1, fp), w_down, fg)


def _qkv_kernel(x_ref, ng_ref, pos_ref, invf_ref, w_ref, o_ref, hn_ref, cos_ref, sin_ref,
                *, tiles_per_kind, head_dim, half, q_scale):
    j = pl.program_id(1)
    tn = o_ref.shape[1]
    pair = head_dim // 2

    def project(hn, cos, sin):
        for c in range(tn // V7X_MXU_DIM):
            acc = _dot(hn, w_ref[:, c * V7X_MXU_DIM:(c + 1) * V7X_MXU_DIM])
            for h in range(V7X_MXU_DIM // head_dim):
                seg = acc[:, h * head_dim:(h + 1) * head_dim]
                col = c * V7X_MXU_DIM + h * head_dim
                o_ref[:, col:col + head_dim] = (seg * cos + pltpu.roll(seg, pair, 1) * sin).astype(BF16)

    @pl.when(j == 0)
    def _():
        hn = _rms(x_ref[...], ng_ref[...]).astype(BF16)
        hn_ref[...] = hn
        ang = pos_ref[...].astype(F32) * invf_ref[...]
        lane = lax.broadcasted_iota(jnp.int32, ang.shape, 1)
        cos = jnp.cos(ang)
        sin = jnp.where(lane < half, -jnp.sin(ang), jnp.sin(ang))
        cos_ref[0] = cos * q_scale
        sin_ref[0] = sin * q_scale
        cos_ref[1] = cos
        sin_ref[1] = sin
        cos_ref[2] = jnp.ones_like(cos)
        sin_ref[2] = jnp.zeros_like(sin)
        project(hn, cos * q_scale, sin * q_scale)

    @pl.when(j > 0)
    def _():
        kind = jnp.minimum(j // tiles_per_kind, 2)
        project(hn_ref[...], cos_ref[kind], sin_ref[kind])


def _rope_layout(head_dim):
    rot = head_dim // 4
    half = rot // 2
    pair = head_dim // 2
    inv_freq = ROPE_THETA ** (-jnp.arange(half, dtype=F32) / half)
    invf = jnp.zeros((head_dim,), F32).at[:half].set(inv_freq).at[pair:pair + half].set(inv_freq)
    return (half, pair, half), invf.reshape(1, head_dim), half


def _qkv(x2, norm_g, positions, w_qkv, invf, *, tm, tn, head_dim, half, qk_width):
    m, d = x2.shape
    n = w_qkv.shape[1]
    assert head_dim == V7X_LANES and qk_width % tn == 0 and n % tn == 0 and tn % V7X_MXU_DIM == 0
    kern = functools.partial(_qkv_kernel, tiles_per_kind=qk_width // tn, head_dim=head_dim, half=half,
                             q_scale=head_dim ** -0.5 * math.log2(math.e))
    return pl.pallas_call(
        kern,
        grid=(m // tm, n // tn),
        in_specs=[
            pl.BlockSpec((tm, d), lambda i, j: (i, 0)),
            pl.BlockSpec((1, d), lambda i, j: (0, 0)),
            pl.BlockSpec((tm, 1), lambda i, j: (i, 0)),
            pl.BlockSpec((1, head_dim), lambda i, j: (0, 0)),
            pl.BlockSpec((d, tn), lambda i, j: (0, j)),
        ],
        out_specs=pl.BlockSpec((tm, tn), lambda i, j: (i, j)),
        out_shape=jax.ShapeDtypeStruct((m, n), BF16),
        scratch_shapes=[pltpu.VMEM((tm, d), BF16)] + [pltpu.VMEM((3, tm, head_dim), F32)] * 2,
        compiler_params=_params("arbitrary", "arbitrary"),
        name="qkv",
    )(x2, norm_g.reshape(1, d), positions.reshape(m, 1), invf, w_qkv)


def _attn_kernel(q_ref, k_ref, v_ref, lq1_ref, lk1_ref, lq2_ref, lk2_ref, sg_ref, o_ref,
                 m_ref, l_ref, acc_ref, sa_ref, sb_ref, ma_ref, mb_ref, *, tq, head_dim, lam_init):
    qi = pl.program_id(2)
    vd = 2 * head_dim
    m_ref[...] = jnp.full(m_ref.shape, MASK_VALUE, F32)
    l_ref[...] = jnp.zeros(l_ref.shape, F32)
    acc_ref[...] = jnp.zeros(acc_ref.shape, F32)

    def scores(kj, buf):
        s_ref, mt_ref = buf
        k0 = pl.multiple_of(kj * tq, tq)
        for idx in range(2):
            cols = slice(idx * head_dim, (idx + 1) * head_dim)
            s = lax.dot_general(q_ref[:, cols], k_ref[pl.ds(k0, tq), cols],
                                (((1,), (1,)), ((), ())), preferred_element_type=F32)
            s_ref[idx] = s
            mt_ref[idx] = jnp.broadcast_to(jnp.max(s, axis=1, keepdims=True), (tq, V7X_LANES))

    def fold(kj, buf, masked=False):
        s_ref, mt_ref = buf
        v = v_ref[pl.ds(pl.multiple_of(kj * tq, tq), tq), :]
        for idx in range(2):
            s = s_ref[idx]
            if masked:
                row = lax.broadcasted_iota(jnp.int32, s.shape, 0)
                col = lax.broadcasted_iota(jnp.int32, s.shape, 1)
                s = jnp.where(col <= row, s, MASK_VALUE)
                m_tile = jnp.max(s, axis=1, keepdims=True)
            else:
                m_tile = mt_ref[idx]
            m_prev = m_ref[idx]
            m_new = jnp.maximum(m_prev, m_tile)
            alpha = jnp.exp2(m_prev - m_new)
            ps = [jnp.exp2(s[:, c * V7X_LANES:(c + 1) * V7X_LANES] - m_new)
                  for c in range(tq // V7X_LANES)]
            l_ref[idx] = alpha * l_ref[idx] + functools.reduce(jnp.add, ps)
            pv = _dot(jnp.concatenate(ps, axis=1).astype(BF16), v)
            alpha_v = jnp.concatenate([alpha] * (vd // V7X_LANES), axis=1)
            acc_ref[idx] = alpha_v * acc_ref[idx] + pv
            m_ref[idx] = m_new

    buf_a = (sa_ref, ma_ref)
    buf_b = (sb_ref, mb_ref)
    scores(0, buf_a)

    def pair(p, carry):
        scores(2 * p + 1, buf_b)
        fold(2 * p, buf_a)
        scores(2 * p + 2, buf_a)
        fold(2 * p + 1, buf_b)
        return carry

    lax.fori_loop(0, lax.shift_right_logical(qi, jnp.int32(1)), pair, 0)

    @pl.when((qi & 1) == 0)
    def _():
        fold(qi, buf_a, masked=True)

    @pl.when((qi & 1) == 1)
    def _():
        scores(qi, buf_b)
        fold(qi - 1, buf_a)
        fold(qi, buf_b, masked=True)

    lam = (jnp.exp(jnp.sum(lq1_ref[...] * lk1_ref[...], keepdims=True))
           - jnp.exp(jnp.sum(lq2_ref[...] * lk2_ref[...], keepdims=True)) + lam_init)
    l0 = jnp.sum(l_ref[0], axis=1, keepdims=True)
    l1 = jnp.sum(l_ref[1], axis=1, keepdims=True)
    o = acc_ref[0] / l0 - lam * (acc_ref[1] / l1)
    o_ref[...] = (_rms(o, sg_ref[...]) * (1.0 - lam_init)).astype(BF16)


def _attn(qkv, lq1, lk1, lq2, lk2, subln_g, *, batch, seq_len, heads, head_dim, tq, lam_init):
    m = qkv.shape[0]
    vd = 2 * head_dim
    nq = seq_len // tq
    k_col0 = heads
    v_col0 = 2 * heads
    vec = lambda a: a.reshape(1, head_dim)
    kern = functools.partial(_attn_kernel, tq=tq, head_dim=head_dim, lam_init=lam_init)
    small = pl.BlockSpec((1, head_dim), lambda b, h, q: (0, 0))
    return pl.pallas_call(
        kern,
        grid=(batch, heads, nq),
        in_specs=[
            pl.BlockSpec((tq, vd), lambda b, h, q: (b * nq + q, h)),
            pl.BlockSpec((seq_len, vd), lambda b, h, q: (b, k_col0 + h)),
            pl.BlockSpec((seq_len, vd), lambda b, h, q: (b, v_col0 + h)),
            small, small, small, small,
            pl.BlockSpec((1, vd), lambda b, h, q: (0, 0)),
        ],
        out_specs=pl.BlockSpec((tq, vd), lambda b, h, q: (b * nq + q, h)),
        out_shape=jax.ShapeDtypeStruct((m, heads * vd), BF16),
        scratch_shapes=[
            pltpu.VMEM((2, tq, V7X_LANES), F32),
            pltpu.VMEM((2, tq, V7X_LANES), F32),
            pltpu.VMEM((2, tq, vd), F32),
            pltpu.VMEM((2, tq, tq), F32),
            pltpu.VMEM((2, tq, tq), F32),
            pltpu.VMEM((2, tq, V7X_LANES), F32),
            pltpu.VMEM((2, tq, V7X_LANES), F32),
        ],
        compiler_params=_params("arbitrary", "arbitrary", "arbitrary"),
        name="diff_attn",
    )(qkv, qkv, qkv, vec(lq1), vec(lk1), vec(lq2), vec(lk2), subln_g.reshape(1, vd))


def _proj_res_kernel(a_ref, w_ref, r_ref, o_ref):
    o_ref[...] = r_ref[...] + _dot(a_ref[...], w_ref[...])


def _proj_res(a, w, res, *, tm):
    m, k = a.shape
    n = w.shape[1]
    return pl.pallas_call(
        _proj_res_kernel,
        grid=(m // tm,),
        in_specs=[
            pl.BlockSpec((tm, k), lambda i: (i, 0)),
            pl.BlockSpec((k, n), lambda i: (0, 0)),
            pl.BlockSpec((tm, n), lambda i: (i, 0)),
        ],
        out_specs=pl.BlockSpec((tm, n), lambda i: (i, 0)),
        out_shape=jax.ShapeDtypeStruct((m, n), F32),
        compiler_params=_params("arbitrary"),
        name="attn_out",
    )(a, w, res)


def _cast_kernel(w_ref, o_ref, *, rows, cols, swap):
    tr, tc = o_ref.shape
    x = w_ref[...]
    if rows % tr or cols % tc:
        r = pl.program_id(0) * tr + lax.broadcasted_iota(jnp.int32, x.shape, 0)
        c = pl.program_id(1) * tc + lax.broadcasted_iota(jnp.int32, x.shape, 1)
        x = jnp.where((r < rows) & (c < cols), x, 0.0)
    if swap is None:
        o_ref[...] = x.astype(BF16)
        return
    head_dim, lo, hi, n, limit = swap
    in_qk = pl.program_id(1) * tc < limit
    lane = lax.broadcasted_iota(jnp.int32, (tr, head_dim), 1)
    take_hi = (lane >= lo) & (lane < lo + n) & in_qk
    take_lo = (lane >= hi) & (lane < hi + n) & in_qk
    for h in range(tc // head_dim):
        seg = x[:, h * head_dim:(h + 1) * head_dim]
        seg = jnp.where(take_hi, pltpu.roll(seg, head_dim - (hi - lo), 1),
                        jnp.where(take_lo, pltpu.roll(seg, hi - lo, 1), seg))
        o_ref[:, h * head_dim:(h + 1) * head_dim] = seg.astype(BF16)


def _cast_weight(w_stack, layer, *, pad_rows_to=1, pad_cols_to=1, swap=None):
    _, rows, cols = w_stack.shape
    rp = -(-rows // pad_rows_to) * pad_rows_to
    cp = -(-cols // pad_cols_to) * pad_cols_to
    tc = _pick(cp, (2048, 1024, 512, 256, 128))
    tr = _pick(rp, tuple(t for t in (2048, 1024, 512, 256, 128, 64, 32, 16) if t * tc <= CAST_BLOCK_ELEMS))
    if swap is not None:
        assert swap[4] % tc == 0 and tc % swap[0] == 0
    kern = functools.partial(_cast_kernel, rows=rows, cols=cols, swap=swap)
    return pl.pallas_call(
        kern,
        grid=(rp // tr, cp // tc),
        in_specs=[pl.BlockSpec((None, tr, tc), lambda i, j: (layer, jnp.minimum(i, (rows - 1) // tr),
                                                              jnp.minimum(j, (cols - 1) // tc)))],
        out_specs=pl.BlockSpec((tr, tc), lambda i, j: (i, j)),
        out_shape=jax.ShapeDtypeStruct((rp, cp), BF16),
        compiler_params=_params("arbitrary", "arbitrary"),
        name="cast_weight",
    )(w_stack)


def _pad_cols(a, n):
    return a if a.shape[-1] == n else jnp.pad(a, [(0, 0)] * (a.ndim - 1) + [(0, n - a.shape[-1])])


def kernel(x, positions, norm_mix, norm_ffn, norm_final, conv_w_in, conv_b_in, conv_dw, conv_dw_b, conv_ln_g, conv_ln_b, conv_w_out, conv_b_out, attn_w_qkv, attn_lq1, attn_lk1, attn_lq2, attn_lk2, attn_subln_g, attn_w_o, ffn_w_gate, ffn_w_up, ffn_dw, ffn_dw_b, ffn_w_down):
    batch, seq_len, d = x.shape
    depth = norm_mix.shape[0]
    m = batch * seq_len
    d_ff = ffn_w_gate.shape[2]
    head_dim = attn_lq1.shape[1]
    heads = d // (2 * head_dim)
    qk_width = 2 * heads * head_dim

    tm = _pick(seq_len, (1024, 512, 256, 128, 64, 32))
    tm_proj = _pick(seq_len, (1024, 512, 256, 128, 64, 32))
    tm_conv = _pick(seq_len, (256, 128, 64))
    tn_wide = _pick(d, (1024, 512, 256))
    tn_qkv = _pick(qk_width, (1024, 512, 256))
    tq = _pick(seq_len, (1024, 512, 256))
    tf = 2 * V7X_MXU_DIM
    f_pad = -(-d_ff // tf) * tf

    x2 = x.reshape(m, d)
    for i in range(depth):
        j = i // N_MIXERS
        if i % N_MIXERS == 0:
            glu = _conf_in(x2, norm_mix[i], _cast_weight(conv_w_in, j), conv_b_in[j], tm=tm, tn=tn_wide)
            x2 = _conf_out(glu, x2, conv_dw[j], conv_dw_b[j], conv_ln_g[j], conv_ln_b[j],
                           _cast_weight(conv_w_out, j), conv_b_out[j], tm=tm_conv, seq_len=seq_len)
        else:
            lam_init = 0.8 - 0.6 * math.exp(-0.3 * i)
            (lo, hi, n), invf, half = _rope_layout(head_dim)
            w_qkv = _cast_weight(attn_w_qkv, j, swap=(head_dim, lo, hi, n, 2 * qk_width))
            qkv = _qkv(x2, norm_mix[i], positions, w_qkv, invf,
                       tm=tm, tn=tn_qkv, head_dim=head_dim, half=half, qk_width=qk_width)
            o = _attn(qkv, attn_lq1[j], attn_lk1[j], attn_lq2[j], attn_lk2[j], attn_subln_g[j],
                      batch=batch, seq_len=seq_len, heads=heads, head_dim=head_dim, tq=tq,
                      lam_init=lam_init)
            x2 = _proj_res(o, _cast_weight(attn_w_o, j), x2, tm=tm_proj)
        final_g = norm_final if i == depth - 1 else None
        x2 = _ffn(x2, norm_ffn[i],
                  _cast_weight(ffn_w_gate, i, pad_cols_to=tf),
                  _cast_weight(ffn_w_up, i, pad_cols_to=tf),
                  _pad_cols(ffn_dw[i], f_pad), _pad_cols(ffn_dw_b[i], f_pad),
                  _cast_weight(ffn_w_down, i, pad_rows_to=tf),
                  final_g, tm=tm, tf=tf, seq_len=seq_len)
    if depth == 0:
        raise ValueError("depth must be positive")
    return x2.reshape(batch, seq_len, d)
```
